```python
import math
import jax
import jax.numpy as jnp
from jax import lax
import numpy as np

D_MODEL = 1024
BATCH = 4
SEQ = 8192
DEPTH = 1

GRID_W = 64
CTX_LEN = 256
EPS = 1e-6

SSD_WIDTH = 512
SSD_HEADDIM = 64
SSD_HEADS = 8
SSD_GROUPS = 2
SSD_STATE = 128
SSD_CONV = 5
SSD_CHUNK = 128
SSD_CONV_CH = SSD_WIDTH + 2 * SSD_GROUPS * SSD_STATE

HGRN_WIDTH = 512
HGRN_HEADS = 4
HGRN_HEADDIM = 128
HGRN_CHUNK = 64

N_EXPERTS = 16
EXPERT_FF = 1024
CAPACITY_FACTOR = 2

COL_SIZES = (SSD_WIDTH, SSD_CONV_CH, 2 * SSD_HEADS, HGRN_WIDTH, HGRN_WIDTH, HGRN_WIDTH, HGRN_WIDTH, HGRN_WIDTH)
IN_COLS = SSD_WIDTH + SSD_CONV_CH + 2 * SSD_HEADS + 5 * HGRN_WIDTH

kernel_name = 'hymba_ssd_hgrn2_ecmoe_prefix_dit_layer'


def rmsnorm(x, w):
    xf = x.astype(jnp.float32)
    y = xf * lax.rsqrt(jnp.mean(xf * xf, axis=-1, keepdims=True) + EPS)
    return (y * w.astype(jnp.float32)).astype(x.dtype)


def modulate(h, shift, scale):
    return h * (1 + scale) + shift


def flip_seq(t):
    return jnp.flip(t, axis=1)


def split_cols(u):
    points, acc = [], 0
    for s in COL_SIZES[:-1]:
        acc += s
        points.append(acc)
    return jnp.split(u, points, axis=-1)


def dwconv_centred(u, w, b):
    pad = w.shape[0] // 2
    y = lax.conv_general_dilated(u, w[:, None, :], (1,), [(pad, pad)],
                                 dimension_numbers=('NWC', 'WIO', 'NWC'),
                                 feature_group_count=u.shape[-1])
    return y + b


def latent_conv(u, w, b):
    bsz, n, ch = u.shape
    rows = n // GRID_W
    return dwconv_centred(u.reshape(bsz * rows, GRID_W, ch), w, b).reshape(bsz, n, ch)


def ssd_scan(xdt, log_a, bmat, cmat, s0, return_y=True):
    f32 = jnp.float32
    b, l, h, p = xdt.shape
    g, n = bmat.shape[2], bmat.shape[3]
    hg = h // g
    q = SSD_CHUNK
    c = l // q
    xc = xdt.reshape(b, c, q, g, hg, p).astype(f32)
    ac = log_a.reshape(b, c, q, g, hg).astype(f32)
    bc = bmat.reshape(b, c, q, g, n).astype(f32)
    cc = cmat.reshape(b, c, q, g, n).astype(f32)
    a_cs = jnp.cumsum(ac, axis=2)
    a_tot = a_cs[:, :, -1]
    decay_end = jnp.exp(a_tot[:, :, None] - a_cs)
    states = jnp.einsum('bcsgn,bcsgj,bcsgjp->bcgjpn', bc, decay_end, xc)

    def step(s, inp):
        st, at = inp
        return jnp.exp(at)[..., None, None] * s + st, s

    s_final, s_in = lax.scan(step, s0.reshape(b, g, hg, p, n).astype(f32),
                             (jnp.moveaxis(states, 1, 0), jnp.moveaxis(a_tot, 1, 0)))
    s_final = s_final.reshape(b, h, p, n)
    if not return_y:
        return None, s_final
    s_in = jnp.moveaxis(s_in, 0, 1)
    seg = a_cs[:, :, :, None] - a_cs[:, :, None, :]
    mask = jnp.tril(jnp.ones((q, q), bool))[:, :, None, None]
    lmat = jnp.exp(jnp.where(mask, seg, -jnp.inf))
    cb = jnp.einsum('bclgn,bcsgn->bclsg', cc, bc)
    y_diag = jnp.einsum('bclsg,bclsgj,bcsgjp->bclgjp', cb, lmat, xc)
    y_off = jnp.einsum('bclgn,bcgjpn,bclgj->bclgjp', cc, s_in, jnp.exp(a_cs))
    return (y_diag + y_off).reshape(b, l, h, p), s_final


def hgrn2_scan(q, f, v, s0, return_y=True):
    f32 = jnp.float32
    b, l, h, dk = q.shape
    dv = v.shape[-1]
    cq = HGRN_CHUNK
    c = l // cq
    qc = q.reshape(b, c, cq, h, dk).astype(f32)
    fc = f.reshape(b, c, cq, h, dk).astype(f32)
    vc = v.reshape(b, c, cq, h, dv).astype(f32)
    kc = 1 - fc
    bcs = jnp.cumsum(jnp.log(fc), axis=2)
    btot = bcs[:, :, -1]
    states = jnp.einsum('bcshk,bcshv->bchkv', kc * jnp.exp(btot[:, :, None] - bcs), vc)

    def step(s, inp):
        st, dt = inp
        return jnp.exp(dt)[..., None] * s + st, s

    s_final, s_in = lax.scan(step, s0.astype(f32),
                             (jnp.moveaxis(states, 1, 0), jnp.moveaxis(btot, 1, 0)))
    if not return_y:
        return None, s_final
    s_in = jnp.moveaxis(s_in, 0, 1)
    q_dec = qc * jnp.exp(bcs)
    k_inv = kc * jnp.exp(-bcs)
    att = jnp.einsum('bclhk,bcshk->bchls', q_dec, k_inv)
    att = jnp.where(jnp.tril(jnp.ones((cq, cq), bool)), att, 0.0)
    o = jnp.einsum('bchls,bcshv->bclhv', att, vc) + jnp.einsum('bclhk,bchkv->bclhv', q_dec, s_in)
    return o.reshape(b, l, h, dv), s_final


def token_mixer(u_lat, u_ctx, conv_w, conv_b, dt_bias, a_log, d_skip, ssd_nw, lb, hgrn_nw, ctx_out):
    f32 = jnp.float32
    z_l, xbc_l, dt_l, q_l, ff_l, fb_l, i_l, g_l = split_cols(u_lat)
    z_c, xbc_c, dt_c, q_c, ff_c, fb_c, i_c, g_c = split_cols(u_ctx)
    bsz = u_lat.shape[0]
    a_neg = -jnp.exp(a_log.astype(f32))
    d32 = d_skip.astype(f32)

    def ssd_inputs(xbc, dt_raw):
        b, l = xbc.shape[:2]
        xs, bm, cm = jnp.split(jax.nn.silu(xbc.astype(f32)),
                               [SSD_WIDTH, SSD_WIDTH + SSD_GROUPS * SSD_STATE], axis=-1)
        xs = xs.reshape(b, l, SSD_HEADS, SSD_HEADDIM)
        bm = bm.reshape(b, l, SSD_GROUPS, SSD_STATE)
        cm = cm.reshape(b, l, SSD_GROUPS, SSD_STATE)
        dt = jax.nn.softplus(dt_raw.astype(f32).reshape(b, l, 2, SSD_HEADS) + dt_bias.astype(f32))
        xdt = xs[:, :, None] * dt[..., None]
        return xs, bm, cm, xdt, dt * a_neg

    def ssd_output(y_f, y_b, xs, z):
        b, l = xs.shape[:2]
        y = y_f + flip_seq(y_b) + d32[:, None] * xs
        y = y.reshape(b, l, SSD_WIDTH) * jax.nn.silu(z.astype(f32))
        y = rmsnorm(y.reshape(b, l, SSD_GROUPS, -1), ssd_nw.reshape(SSD_GROUPS, -1))
        return y.reshape(b, l, SSD_WIDTH)

    def hgrn_inputs(q_raw, f_raw_f, f_raw_b, i_raw):
        b, l = q_raw.shape[:2]
        shp = (b, l, HGRN_HEADS, HGRN_HEADDIM)
        qq = jax.nn.silu(q_raw.astype(f32)).reshape(shp)
        f_f = (lb[0] + (1 - lb[0]) * jax.nn.sigmoid(f_raw_f.astype(f32))).reshape(shp)
        f_b = (lb[1] + (1 - lb[1]) * jax.nn.sigmoid(f_raw_b.astype(f32))).reshape(shp)
        return qq, f_f, f_b, i_raw.astype(f32).reshape(shp)

    def hgrn_output(o_f, o_b, g):
        b, l = g.shape[:2]
        o = rmsnorm(o_f + flip_seq(o_b), hgrn_nw.reshape(HGRN_HEADS, HGRN_HEADDIM))
        return o.reshape(b, l, HGRN_WIDTH) * jax.nn.silu(g.astype(f32))

    xs_c, b_c, c_c, xdt_c, la_c = ssd_inputs(dwconv_centred(xbc_c, conv_w, conv_b), dt_c)
    xs_l, b_l, c_l, xdt_l, la_l = ssd_inputs(latent_conv(xbc_l, conv_w, conv_b), dt_l)
    s0 = jnp.zeros((bsz, SSD_HEADS, SSD_HEADDIM, SSD_STATE), f32)
    yc_f, sc_f = ssd_scan(xdt_c[:, :, 0], la_c[:, :, 0], b_c, c_c, s0, ctx_out)
    yc_b, sc_b = ssd_scan(flip_seq(xdt_c[:, :, 1]), flip_seq(la_c[:, :, 1]),
                          flip_seq(b_c), flip_seq(c_c), s0, ctx_out)
    yl_f, _ = ssd_scan(xdt_l[:, :, 0], la_l[:, :, 0], b_l, c_l, sc_f)
    yl_b, _ = ssd_scan(flip_seq(xdt_l[:, :, 1]), flip_seq(la_l[:, :, 1]),
                       flip_seq(b_l), flip_seq(c_l), sc_b)
    ssd_lat = ssd_output(yl_f, yl_b, xs_l, z_l)

    qh_c, f_fc, f_bc, v_c = hgrn_inputs(q_c, ff_c, fb_c, i_c)
    qh_l, f_fl, f_bl, v_l = hgrn_inputs(q_l, ff_l, fb_l, i_l)
    r0 = jnp.zeros((bsz, HGRN_HEADS, HGRN_HEADDIM, HGRN_HEADDIM), f32)
    oc_f, rc_f = hgrn2_scan(qh_c, f_fc, v_c, r0, ctx_out)
    oc_b, rc_b = hgrn2_scan(flip_seq(qh_c), flip_seq(f_bc), flip_seq(v_c), r0, ctx_out)
    ol_f, _ = hgrn2_scan(qh_l, f_fl, v_l, rc_f)
    ol_b, _ = hgrn2_scan(flip_seq(qh_l), flip_seq(f_bl), flip_seq(v_l), rc_b)
    hgrn_lat = hgrn_output(ol_f, ol_b, g_l)

    y_lat = jnp.concatenate([ssd_lat, hgrn_lat], axis=-1).astype(u_lat.dtype)
    y_ctx = None
    if ctx_out:
        y_ctx = jnp.concatenate([ssd_output(yc_f, yc_b, xs_c, z_c),
                                 hgrn_output(oc_f, oc_b, g_c)], axis=-1).astype(u_ctx.dtype)
    return y_lat, y_ctx


def expert_choice_moe(h, w_router, w_gate, w_up, w_down):
    n, d = h.shape[1], h.shape[2]
    cap = CAPACITY_FACTOR * n // N_EXPERTS
    aff = jax.nn.softmax(jnp.einsum('bnd,de->ben', h, w_router).astype(jnp.float32), axis=1)
    gate, idx = lax.top_k(aff, cap)
    xin = jax.vmap(lambda hb, ib: hb[ib])(h, idx)
    hg = jnp.einsum('becd,edf->becf', xin, w_gate)
    hu = jnp.einsum('becd,edf->becf', xin, w_up)
    y = jnp.einsum('becf,efd->becd', jax.nn.silu(hg) * hu, w_down)
    y = y * gate[..., None].astype(y.dtype)
    return jax.vmap(lambda yb, ib: jnp.zeros((n, d), y.dtype).at[ib.reshape(-1)].add(yb.reshape(-1, d)))(y, idx)


def setup_inputs(seed: int = 0) -> dict:
    key = jax.random.key(seed)
    ks = jax.random.split(key, 24)
    f32 = jnp.float32

    def nrm(k, shape, scale):
        return jax.random.normal(k, shape, f32) * scale

    dt0 = jnp.exp(jax.random.uniform(ks[10], (DEPTH, 2, SSD_HEADS), f32, math.log(1e-3), math.log(1e-1)))
    return {
        'x': nrm(ks[0], (BATCH, SEQ, D_MODEL), 1.0),
        'c': nrm(ks[1], (BATCH, D_MODEL), 1.0),
        'ctx': nrm(ks[2], (BATCH, CTX_LEN, D_MODEL), 1.0),
        'c_ctx': nrm(ks[3], (D_MODEL,), 1.0),
        'ada_w': nrm(ks[4], (DEPTH, D_MODEL, 6 * D_MODEL), 0.5 * D_MODEL ** -0.5),
        'ada_b': nrm(ks[5], (DEPTH, 6 * D_MODEL), 0.02),
        'norm_w': 1.0 + nrm(ks[6], (DEPTH, 4, D_MODEL), 0.02),
        'w_in': nrm(ks[7], (DEPTH, D_MODEL, IN_COLS), D_MODEL ** -0.5),
        'ssd_conv_w': nrm(ks[8], (DEPTH, SSD_CONV, SSD_CONV_CH), SSD_CONV ** -0.5),
        'ssd_conv_b': nrm(ks[9], (DEPTH, SSD_CONV_CH), 0.02),
        'ssd_dt_bias': dt0 + jnp.log(-jnp.expm1(-dt0)),
        'ssd_a_log': jnp.log(jax.random.uniform(ks[11], (DEPTH, 2, SSD_HEADS), f32, 1.0, 16.0)),
        'ssd_d': 1.0 + nrm(ks[12], (DEPTH, SSD_HEADS), 0.02),
        'ssd_norm_w': 1.0 + nrm(ks[13], (DEPTH, SSD_WIDTH), 0.02),
        'hgrn_lb': nrm(ks[14], (DEPTH + 1, 2, HGRN_WIDTH), 0.1),
        'hgrn_norm_w': 1.0 + nrm(ks[15], (DEPTH, HGRN_WIDTH), 0.02),
        'w_out': nrm(ks[16], (DEPTH, D_MODEL, D_MODEL), D_MODEL ** -0.5),
        'w_router': nrm(ks[17], (DEPTH, D_MODEL, N_EXPERTS), D_MODEL ** -0.5),
        'w_gate': nrm(ks[18], (DEPTH, N_EXPERTS, D_MODEL, EXPERT_FF), D_MODEL ** -0.5),
        'w_up': nrm(ks[19], (DEPTH, N_EXPERTS, D_MODEL, EXPERT_FF), D_MODEL ** -0.5),
        'w_down': nrm(ks[20], (DEPTH, N_EXPERTS, EXPERT_FF, D_MODEL), EXPERT_FF ** -0.5),
    }


def reference(x, c, ctx, c_ctx, ada_w, ada_b, norm_w, w_in, ssd_conv_w, ssd_conv_b, ssd_dt_bias,
              ssd_a_log, ssd_d, ssd_norm_w, hgrn_lb, hgrn_norm_w, w_out, w_router, w_gate, w_up, w_down):
    lower_bounds = jnp.cumsum(jax.nn.softmax(hgrn_lb.astype(jnp.float32), axis=0), axis=0)
    x_lat, x_ctx = x, ctx
    for layer in range(DEPTH):
        ctx_out = layer < DEPTH - 1
        nw = norm_w[layer]
        mod_lat = jnp.split((jax.nn.silu(c) @ ada_w[layer] + ada_b[layer])[:, None, :], 6, axis=-1)
        mod_ctx = jnp.split(jax.nn.silu(c_ctx) @ ada_w[layer] + ada_b[layer], 6, axis=-1)

        h_lat = modulate(rmsnorm(x_lat, nw[0]), mod_lat[0], mod_lat[1])
        h_ctx = modulate(rmsnorm(x_ctx, nw[0]), mod_ctx[0], mod_ctx[1])
        y_lat, y_ctx = token_mixer(h_lat @ w_in[layer], h_ctx @ w_in[layer], ssd_conv_w[layer],
                                   ssd_conv_b[layer], ssd_dt_bias[layer], ssd_a_log[layer], ssd_d[layer],
                                   ssd_norm_w[layer], lower_bounds[layer], hgrn_norm_w[layer], ctx_out)
        x_lat = x_lat + mod_lat[2] * rmsnorm(y_lat @ w_out[layer], nw[1])

        h_lat = modulate(rmsnorm(x_lat, nw[2]), mod_lat[3], mod_lat[4])
        moe_lat = expert_choice_moe(h_lat, w_router[layer], w_gate[layer], w_up[layer], w_down[layer])
        x_lat = x_lat + mod_lat[5] * rmsnorm(moe_lat, nw[3])

        if ctx_out:
            x_ctx = x_ctx + mod_ctx[2] * rmsnorm(y_ctx @ w_out[layer], nw[1])
            h_ctx = modulate(rmsnorm(x_ctx, nw[2]), mod_ctx[3], mod_ctx[4])
            moe_ctx = expert_choice_moe(h_ctx, w_router[layer], w_gate[layer], w_up[layer], w_down[layer])
            x_ctx = x_ctx + mod_ctx[5] * rmsnorm(moe_ctx, nw[3])
    return x_lat
```

```python
import functools

import jax
import jax.numpy as jnp
from jax import lax
from jax.experimental import pallas as pl
from jax.experimental.pallas import tpu as pltpu

F32 = jnp.float32
BF16 = jnp.bfloat16
EPS = 1e-6

LANES = 128
SUBLANES = 8
BF16_ROWS = 16
VMEM_LIMIT = 56 * 1024 * 1024

GRID_W = 64
SSD_WIDTH, SSD_HEADS, SSD_HEADDIM, SSD_GROUPS, SSD_STATE = 512, 8, 64, 2, 128
HGRN_WIDTH, HGRN_HEADS, HGRN_HEADDIM, HGRN_CHUNK = 512, 4, 128, 64
XBC = SSD_WIDTH + 2 * SSD_GROUPS * SSD_STATE
HG = SSD_HEADS // SSD_GROUPS
GW = HG * SSD_HEADDIM
N_EXPERTS = 16
CAPACITY_FACTOR = 2

SEC_XBC, SEC_Q, SEC_V, SEC_FF, SEC_FB, SEC_Z, SEC_G, SEC_DT = 0, 1024, 1536, 2048, 2560, 3072, 3584, 4096
U_COLS = 4096 + LANES

Q = 128
TM = 256
TC = 128
WIN = TC + BF16_ROWS


def _dot(a, b):
    return jnp.dot(a, b, preferred_element_type=F32)


def _dot_nt(a, b):
    return lax.dot_general(a, b, (((1,), (1,)), ((), ())), preferred_element_type=F32)


def _dot_tn(a, b):
    return _dot(a.T.astype(BF16), b)


def _dot01(m01, x):
    x1 = x.astype(BF16)
    r1 = x - x1.astype(F32)
    x2 = r1.astype(BF16)
    x3 = (r1 - x2.astype(F32)).astype(BF16)
    return _dot(m01, x1) + _dot(m01, x2) + _dot(m01, x3)


def _silu(x):
    return x * jax.nn.sigmoid(x)


def _rms(x, w):
    return x * lax.rsqrt(jnp.mean(x * x, axis=-1, keepdims=True) + EPS) * w


def _ada_kernel(c_ref, w_ref, b_ref, o_ref):
    c = c_ref[...]
    o_ref[...] = jnp.dot(_silu(c), w_ref[...], precision=lax.Precision.HIGHEST,
                         preferred_element_type=F32) + b_ref[...]


def _ada(cvec, w, b):
    rows, d = cvec.shape
    n = w.shape[1]
    tn = 1024
    return pl.pallas_call(
        _ada_kernel,
        out_shape=jax.ShapeDtypeStruct((rows, n), F32),
        grid=(n // tn,),
        in_specs=[pl.BlockSpec((rows, d), lambda j: (0, 0)),
                  pl.BlockSpec((d, tn), lambda j: (0, j)),
                  pl.BlockSpec((1, tn), lambda j: (0, j))],
        out_specs=pl.BlockSpec((rows, tn), lambda j: (0, j)),
        compiler_params=pltpu.CompilerParams(dimension_semantics=("arbitrary",), vmem_limit_bytes=VMEM_LIMIT),
        name="ada",
    )(cvec, w, b)


def _inproj_kernel(x_ref, sh_ref, sc_ref, nw_ref, w_ref, cw_ref, cb_ref, dtb_ref, lbf_ref, lbb_ref,
                   u_ref, *, row_len):
    x = x_ref[0]
    h = _rms(x, nw_ref[...]) * (1.0 + sc_ref[0]) + sh_ref[0]
    hb = h.astype(BF16)
    tm = x.shape[0]

    xbc = _dot(hb, w_ref[:, SEC_XBC:SEC_XBC + XBC])
    pos = lax.broadcasted_iota(jnp.int32, (tm, 1), 0) % row_len
    cw = cw_ref[...]
    acc = cb_ref[...] + cw[2:3, :] * xbc
    for d in (-2, -1, 1, 2):
        shifted = pltpu.roll(xbc, (-d) % tm, 0)
        ok = jnp.logical_and(pos + d >= 0, pos + d < row_len)
        acc = acc + cw[2 + d:3 + d, :] * jnp.where(ok, shifted, 0.0)
    u_ref[0, :, SEC_XBC:SEC_XBC + XBC] = _silu(acc)

    u_ref[0, :, SEC_Q:SEC_Q + 512] = _silu(_dot(hb, w_ref[:, SEC_Q:SEC_Q + 512]))
    u_ref[0, :, SEC_V:SEC_V + 512] = _dot(hb, w_ref[:, SEC_V:SEC_V + 512])
    lbf = lbf_ref[...]
    u_ref[0, :, SEC_FF:SEC_FF + 512] = lbf + (1.0 - lbf) * jax.nn.sigmoid(_dot(hb, w_ref[:, SEC_FF:SEC_FF + 512]))
    lbb = lbb_ref[...]
    u_ref[0, :, SEC_FB:SEC_FB + 512] = lbb + (1.0 - lbb) * jax.nn.sigmoid(_dot(hb, w_ref[:, SEC_FB:SEC_FB + 512]))
    u_ref[0, :, SEC_Z:SEC_Z + 512] = _silu(_dot(hb, w_ref[:, SEC_Z:SEC_Z + 512]))
    u_ref[0, :, SEC_G:SEC_G + 512] = _silu(_dot(hb, w_ref[:, SEC_G:SEC_G + 512]))
    dtr = _dot(hb, w_ref[:, SEC_DT:SEC_DT + LANES]) + dtb_ref[...]
    u_ref[0, :, SEC_DT:SEC_DT + LANES] = jnp.maximum(dtr, 0.0) + jnp.log(1.0 + jnp.exp(-jnp.abs(dtr)))


def _inproj(x, shift, scale, nw, w_r, cw, cb, dtb, lbf, lbb, row_len):
    b, l, d = x.shape
    tm = min(TM, l)
    vec = lambda n: pl.BlockSpec((1, n), lambda i, j: (0, 0))
    return pl.pallas_call(
        functools.partial(_inproj_kernel, row_len=row_len),
        out_shape=jax.ShapeDtypeStruct((b, l, U_COLS), F32),
        grid=(b, l // tm),
        in_specs=[pl.BlockSpec((1, tm, d), lambda i, j: (i, j, 0)),
                  pl.BlockSpec((1, 1, d), lambda i, j: (i, 0, 0)),
                  pl.BlockSpec((1, 1, d), lambda i, j: (i, 0, 0)),
                  vec(d),
                  pl.BlockSpec((d, U_COLS), lambda i, j: (0, 0)),
                  pl.BlockSpec((5, XBC), lambda i, j: (0, 0)),
                  vec(XBC), vec(LANES), vec(512), vec(512)],
        out_specs=pl.BlockSpec((1, tm, U_COLS), lambda i, j: (i, j, 0)),
        compiler_params=pltpu.CompilerParams(dimension_semantics=("arbitrary", "arbitrary"),
                                             vmem_limit_bytes=VMEM_LIMIT),
        name="inproj",
    )(x, shift, scale, nw, w_r, cw, cb, dtb, lbf, lbb)


def _scan_dir(d, xbc_ref, q_ref, v_ref, f_ref, dt_ref, a_neg, y_ref, ss_ref, sh_ref):
    row = lax.broadcasted_iota(jnp.int32, (Q, Q), 0)
    col = lax.broadcasted_iota(jnp.int32, (Q, Q), 1)
    tri = (col <= row) if d == 0 else (col >= row)
    blk = jnp.logical_and(tri, (row // HGRN_CHUNK) == (col // HGRN_CHUNK))
    tri_bf = tri.astype(BF16)
    blk_bf = blk.astype(BF16)

    xbc = xbc_ref[0]
    xs = xbc[:, :SSD_WIDTH]
    bm = xbc[:, SSD_WIDTH:SSD_WIDTH + SSD_GROUPS * SSD_STATE]
    cm = xbc[:, SSD_WIDTH + SSD_GROUPS * SSD_STATE:]
    dt = dt_ref[0]
    a_cs = _dot01(tri_bf, dt * a_neg)
    a_cs_t = a_cs.T
    tot = Q - 1 if d == 0 else 0
    a_tot = a_cs[tot:tot + 1, :]
    for g in range(SSD_GROUPS):
        bg = bm[:, g * SSD_STATE:(g + 1) * SSD_STATE].astype(BF16)
        cg = cm[:, g * SSD_STATE:(g + 1) * SSD_STATE].astype(BF16)
        cb = _dot_nt(cg, bg)
        s_in = ss_ref[0, d, g]
        y_off = _dot_nt(cg, s_in.astype(BF16))
        ys, xdecs, dtots = [], [], []
        for j in range(HG):
            h = g * HG + j
            hl = d * SSD_HEADS + h
            a_col = a_cs[:, hl:hl + 1]
            a_row = a_cs_t[hl:hl + 1, :]
            lmat = jnp.exp(jnp.where(tri, a_col - a_row, -jnp.inf))
            xdt = xs[:, h * SSD_HEADDIM:(h + 1) * SSD_HEADDIM] * dt[:, hl:hl + 1]
            y_diag = _dot((cb * lmat).astype(BF16), xdt.astype(BF16))
            ys.append(y_diag + y_off[:, j * SSD_HEADDIM:(j + 1) * SSD_HEADDIM] * jnp.exp(a_col))
            a_t = a_tot[:, hl:hl + 1]
            xdecs.append(xdt * jnp.exp(a_t - a_col))
            dtots.append(jnp.broadcast_to(jnp.exp(a_t), (SSD_HEADDIM, SSD_STATE)))
        st = _dot_tn(jnp.concatenate(xdecs, axis=1), bg)
        ss_ref[0, d, g] = jnp.concatenate(dtots, axis=0) * s_in + st
        y_ref[0, :, g * GW:(g + 1) * GW] = jnp.concatenate(ys, axis=1)

    qq = q_ref[0]
    vv = v_ref[0]
    ff = f_ref[0]
    bcs = _dot01(blk_bf, jnp.log(ff))
    q_dec = qq * jnp.exp(bcs)
    kk = 1.0 - ff
    k_inv = (kk * jnp.exp(-bcs)).astype(BF16)
    if d == 0:
        tots = (bcs[HGRN_CHUNK - 1:HGRN_CHUNK, :], bcs[Q - 1:Q, :])
    else:
        tots = (bcs[0:1, :], bcs[HGRN_CHUNK:HGRN_CHUNK + 1, :])
    tot_full = jnp.concatenate([jnp.broadcast_to(tots[0], (HGRN_CHUNK, HGRN_WIDTH)),
                                jnp.broadcast_to(tots[1], (HGRN_CHUNK, HGRN_WIDTH))], axis=0)
    k_end = kk * jnp.exp(tot_full - bcs)
    in_first = lax.broadcasted_iota(jnp.int32, (Q, 1), 0) < HGRN_CHUNK
    k_ends = (jnp.where(in_first, k_end, 0.0).astype(BF16), jnp.where(in_first, 0.0, k_end).astype(BF16))
    q_dec_bf = q_dec.astype(BF16)
    vv_bf = vv.astype(BF16)
    order = (0, 1) if d == 0 else (1, 0)
    for h in range(HGRN_HEADS):
        sl = slice(h * HGRN_HEADDIM, (h + 1) * HGRN_HEADDIM)
        att = jnp.where(blk, _dot_nt(q_dec_bf[:, sl], k_inv[:, sl]), 0.0)
        o_intra = _dot(att.astype(BF16), vv_bf[:, sl])
        s_t = sh_ref[0, d, h]
        v_t = vv[:, sl].T.astype(BF16)
        outs = [None, None]
        for ci in order:
            r0, r1 = ci * HGRN_CHUNK, (ci + 1) * HGRN_CHUNK
            outs[ci] = o_intra[r0:r1] + _dot_nt(q_dec_bf[r0:r1, sl], s_t.astype(BF16))
            s_t = jnp.exp(tots[ci][:, sl]) * s_t + _dot(v_t, k_ends[ci][:, sl])
        sh_ref[0, d, h] = s_t
        y_ref[0, :, SSD_WIDTH + h * HGRN_HEADDIM:SSD_WIDTH + (h + 1) * HGRN_HEADDIM] = jnp.concatenate(outs, axis=0)


def _mix_kernel(xbc_f, q_f, v_f, f_f, dt_f, xbc_b, q_b, v_b, f_b, dt_b, alog_ref, s0s_ref, s0h_ref,
                yf_ref, yb_ref, ss_ref, sh_ref):
    @pl.when(pl.program_id(1) == 0)
    def _():
        ss_ref[...] = s0s_ref[...]
        sh_ref[...] = s0h_ref[...]

    a_neg = -jnp.exp(alog_ref[...])
    _scan_dir(0, xbc_f, q_f, v_f, f_f, dt_f, a_neg, yf_ref, ss_ref, sh_ref)
    _scan_dir(1, xbc_b, q_b, v_b, f_b, dt_b, a_neg, yb_ref, ss_ref, sh_ref)


def _mix(u, alog, s0s, s0h):
    b, l, _ = u.shape
    nc = l // Q
    fwd = lambda w, sec: pl.BlockSpec((1, Q, w), lambda i, c: (i, c, sec // w))
    bwd = lambda w, sec: pl.BlockSpec((1, Q, w), lambda i, c: (i, nc - 1 - c, sec // w))
    ss_spec = pl.BlockSpec((1,) + s0s.shape[1:], lambda i, c: (i, 0, 0, 0, 0))
    sh_spec = pl.BlockSpec((1,) + s0h.shape[1:], lambda i, c: (i, 0, 0, 0, 0))
    return pl.pallas_call(
        _mix_kernel,
        out_shape=(jax.ShapeDtypeStruct((b, l, 2 * SSD_WIDTH), F32),
                   jax.ShapeDtypeStruct((b, l, 2 * SSD_WIDTH), F32),
                   jax.ShapeDtypeStruct(s0s.shape, F32),
                   jax.ShapeDtypeStruct(s0h.shape, F32)),
        grid=(b, nc),
        in_specs=[fwd(XBC, SEC_XBC), fwd(512, SEC_Q), fwd(512, SEC_V), fwd(512, SEC_FF), fwd(LANES, SEC_DT),
                  bwd(XBC, SEC_XBC), bwd(512, SEC_Q), bwd(512, SEC_V), bwd(512, SEC_FB), bwd(LANES, SEC_DT),
                  pl.BlockSpec((1, LANES), lambda i, c: (0, 0)), ss_spec, sh_spec],
        out_specs=(pl.BlockSpec((1, Q, 2 * SSD_WIDTH), lambda i, c: (i, c, 0)),
                   pl.BlockSpec((1, Q, 2 * SSD_WIDTH), lambda i, c: (i, nc - 1 - c, 0)),
                   ss_spec, sh_spec),
        compiler_params=pltpu.CompilerParams(dimension_semantics=("arbitrary", "arbitrary"),
                                             vmem_limit_bytes=VMEM_LIMIT),
        name="mix",
    )(u, u, u, u, u, u, u, u, u, u, alog, s0s, s0h)


def _outproj_kernel(x_ref, yf_ref, yb_ref, xs_ref, z_ref, g_ref, dexp_ref, snw_ref, hnw_ref, wout_ref,
                    nw1_ref, nw2_ref, g1_ref, sh2_ref, sc2_ref, wrt_ref,
                    x1_ref, h2_ref, aff_ref):
    tm = x_ref.shape[1]
    y = yf_ref[0] + yb_ref[0]
    ys = (y[:, :SSD_WIDTH] + dexp_ref[...] * xs_ref[0]) * z_ref[0]
    snw = snw_ref[...]
    parts = []
    for g in range(SSD_GROUPS):
        parts.append(_rms(ys[:, g * GW:(g + 1) * GW], snw[:, g * GW:(g + 1) * GW]))
    yh = y[:, SSD_WIDTH:]
    hnw = hnw_ref[...]
    gate = g_ref[0]
    for h in range(HGRN_HEADS):
        sl = slice(h * HGRN_HEADDIM, (h + 1) * HGRN_HEADDIM)
        parts.append(_rms(yh[:, sl], hnw[:, sl]) * gate[:, sl])
    ycat = jnp.concatenate(parts, axis=1).astype(BF16)
    proj = _dot(ycat, wout_ref[...])
    x1 = x_ref[0] + g1_ref[0] * _rms(proj, nw1_ref[...])
    x1_ref[0] = x1
    h2 = _rms(x1, nw2_ref[...]) * (1.0 + sc2_ref[0]) + sh2_ref[0]
    for s in range(SUBLANES):
        h2_ref[pl.ds(s, tm, stride=SUBLANES), :] = h2[:, s * LANES:(s + 1) * LANES]
    logits = lax.dot_general(wrt_ref[...], h2, (((1,), (1,)), ((), ())), precision=lax.Precision.HIGHEST,
                             preferred_element_type=F32)
    m = jnp.max(logits, axis=0, keepdims=True)
    p = jnp.exp(logits - m)
    aff_ref[0] = p / jnp.sum(p, axis=0, keepdims=True)


def _outproj(x, yf, yb, u, dexp, snw, hnw, wout, nw1, nw2, g1, sh2, sc2, wrt):
    b, l, d = x.shape
    tm = TM
    nt = l // tm
    vec = lambda n: pl.BlockSpec((1, n), lambda i, j: (0, 0))
    mod = pl.BlockSpec((1, 1, d), lambda i, j: (i, 0, 0))
    usec = lambda sec: pl.BlockSpec((1, tm, 512), lambda i, j: (i, j, sec // 512))
    return pl.pallas_call(
        _outproj_kernel,
        out_shape=(jax.ShapeDtypeStruct((b, l, d), F32),
                   jax.ShapeDtypeStruct((b * l * SUBLANES, LANES), F32),
                   jax.ShapeDtypeStruct((b, N_EXPERTS, l), F32)),
        grid=(b, nt),
        in_specs=[pl.BlockSpec((1, tm, d), lambda i, j: (i, j, 0)),
                  pl.BlockSpec((1, tm, 2 * SSD_WIDTH), lambda i, j: (i, j, 0)),
                  pl.BlockSpec((1, tm, 2 * SSD_WIDTH), lambda i, j: (i, j, 0)),
                  usec(SEC_XBC), usec(SEC_Z), usec(SEC_G),
                  vec(512), vec(512), vec(512),
                  pl.BlockSpec((d, d), lambda i, j: (0, 0)),
                  vec(d), vec(d), mod, mod, mod,
                  pl.BlockSpec((N_EXPERTS, d), lambda i, j: (0, 0))],
        out_specs=(pl.BlockSpec((1, tm, d), lambda i, j: (i, j, 0)),
                   pl.BlockSpec((tm * SUBLANES, LANES), lambda i, j: (i * nt + j, 0)),
                   pl.BlockSpec((1, N_EXPERTS, tm), lambda i, j: (i, 0, j))),
        compiler_params=pltpu.CompilerParams(dimension_semantics=("arbitrary", "arbitrary"),
                                             vmem_limit_bytes=VMEM_LIMIT),
        name="outproj",
    )(x, yf, yb, u, u, u, dexp, snw, hnw, wout, nw1, nw2, g1, sh2, sc2, wrt)


def _expert_kernel(idx_ref, h2_hbm, wg_ref, wu_ref, wd_ref, y_ref, xg, wgb, wub, wdb, sem, *, n_tok, cap):
    b = pl.program_id(1)

    @pl.when(b == 0)
    def _():
        wgb[...] = wg_ref[0].astype(BF16)
        wub[...] = wu_ref[0].astype(BF16)
        wdb[...] = wd_ref[0].astype(BF16)

    def row_copy(p):
        tok = b * n_tok + idx_ref[0, 0, p]
        return pltpu.make_async_copy(h2_hbm.at[tok], xg.at[pl.ds(pl.multiple_of(p * SUBLANES, SUBLANES), SUBLANES), :], sem)

    def issue(p, carry):
        row_copy(p).start()
        return carry

    def drain(p, carry):
        row_copy(p).wait()
        return carry

    lax.fori_loop(0, cap, issue, 0)
    lax.fori_loop(0, cap, drain, 0)

    rc = 256
    for r in range(cap // rc):
        xr = jnp.concatenate(
            [xg[pl.ds(r * rc * SUBLANES + s, rc, stride=SUBLANES), :] for s in range(SUBLANES)], axis=1).astype(BF16)
        hg = _dot(xr, wgb[...])
        hu = _dot(xr, wub[...])
        act = (_silu(hg) * hu).astype(BF16)
        y_ref[0, 0, r * rc:(r + 1) * rc, :] = _dot(act, wdb[...]).astype(BF16)


def _expert(idx, h2r, wg, wu, wd, b, n_tok):
    e, d, f = wg.shape
    cap = idx.shape[-1]
    return pl.pallas_call(
        functools.partial(_expert_kernel, n_tok=n_tok, cap=cap),
        out_shape=jax.ShapeDtypeStruct((b, e, cap, d), BF16),
        grid=(e, b),
        in_specs=[pl.BlockSpec((1, 1, cap), lambda i, j: (j * e + i, 0, 0), memory_space=pltpu.SMEM),
                  pl.BlockSpec(memory_space=pl.ANY),
                  pl.BlockSpec((1, d, f), lambda i, j: (i, 0, 0)),
                  pl.BlockSpec((1, d, f), lambda i, j: (i, 0, 0)),
                  pl.BlockSpec((1, f, d), lambda i, j: (i, 0, 0))],
        out_specs=pl.BlockSpec((1, 1, cap, d), lambda i, j: (j, i, 0, 0)),
        scratch_shapes=[pltpu.VMEM((cap * SUBLANES, LANES), F32),
                        pltpu.VMEM((d, f), BF16), pltpu.VMEM((d, f), BF16), pltpu.VMEM((f, d), BF16),
                        pltpu.SemaphoreType.DMA(())],
        compiler_params=pltpu.CompilerParams(dimension_semantics=("arbitrary", "arbitrary"),
                                             vmem_limit_bytes=VMEM_LIMIT),
        name="expert",
    )(idx, h2r, wg, wu, wd)


def _combine_kernel(st_ref, x1_ref, pos_ref, aff_ref, y_hbm, nw_ref, g2_ref, o_ref, ywin, sem, *, nt, cap):
    i = pl.program_id(0)
    j = pl.program_id(1)
    step = i * nt + j
    total = pl.num_programs(0) * nt
    slot = step % 2

    def win_start(s, e):
        st = st_ref[s * N_EXPERTS + e]
        st = lax.shift_left(lax.shift_right_logical(st, 4), 4)
        return pl.multiple_of(jnp.minimum(st, cap - WIN), BF16_ROWS)

    def win_copy(bi, s, sl, e):
        return pltpu.make_async_copy(y_hbm.at[bi, e, pl.ds(win_start(s, e), WIN), :], ywin.at[sl, e], sem.at[sl])

    @pl.when(step == 0)
    def _():
        for e in range(N_EXPERTS):
            win_copy(i, step, slot, e).start()

    @pl.when(step + 1 < total)
    def _():
        nxt_b = jnp.where(j + 1 == nt, i + 1, i)
        for e in range(N_EXPERTS):
            win_copy(nxt_b, step + 1, 1 - slot, e).start()

    for e in range(N_EXPERTS):
        win_copy(i, step, slot, e).wait()

    pos = pos_ref[0]
    aff = aff_ref[0]
    lane = lax.broadcasted_iota(jnp.int32, (1, WIN), 1)
    acc = jnp.zeros((TC, o_ref.shape[2]), F32)
    for e in range(N_EXPERTS):
        onehot = (pos[:, e:e + 1] - win_start(step, e) == lane).astype(BF16)
        acc = acc + aff[:, e:e + 1] * _dot(onehot, ywin[slot, e])
    o_ref[0] = x1_ref[0] + g2_ref[0] * _rms(acc, nw_ref[...])


def _combine(starts, x1, pos_t, aff_t, y, nw, g2):
    b, l, d = x1.shape
    nt = l // TC
    cap = y.shape[2]
    grid_spec = pltpu.PrefetchScalarGridSpec(
        num_scalar_prefetch=1,
        grid=(b, nt),
        in_specs=[pl.BlockSpec((1, TC, d), lambda i, j, st: (i, j, 0)),
                  pl.BlockSpec((1, TC, N_EXPERTS), lambda i, j, st: (i, j, 0)),
                  pl.BlockSpec((1, TC, N_EXPERTS), lambda i, j, st: (i, j, 0)),
                  pl.BlockSpec(memory_space=pl.ANY),
                  pl.BlockSpec((1, d), lambda i, j, st: (0, 0)),
                  pl.BlockSpec((1, 1, d), lambda i, j, st: (i, 0, 0))],
        out_specs=pl.BlockSpec((1, TC, d), lambda i, j, st: (i, j, 0)),
        scratch_shapes=[pltpu.VMEM((2, N_EXPERTS, WIN, d), BF16), pltpu.SemaphoreType.DMA((2,))],
    )
    return pl.pallas_call(
        functools.partial(_combine_kernel, nt=nt, cap=cap),
        out_shape=jax.ShapeDtypeStruct((b, l, d), F32),
        grid_spec=grid_spec,
        compiler_params=pltpu.CompilerParams(dimension_semantics=("arbitrary", "arbitrary"),
                                             vmem_limit_bytes=VMEM_LIMIT),
        name="combine",
    )(starts, x1, pos_t, aff_t, y, nw, g2)


def _route(aff, cap):
    b, e, n = aff.shape
    _, idx = lax.top_k(aff, cap)
    idx = jnp.sort(idx, axis=-1).astype(jnp.int32)
    bi = jnp.arange(b)[:, None, None]
    ei = jnp.arange(e)[None, :, None]
    sel = jnp.zeros((b, e, n), jnp.int32).at[bi, ei, idx].set(1)
    incl = jnp.cumsum(sel, axis=-1)
    pos = jnp.where(sel > 0, incl - 1, -(1 << 20))
    starts = jnp.concatenate([jnp.zeros((b, e, 1), jnp.int32), incl[:, :, TC - 1:n - 1:TC]], axis=-1)
    starts = jnp.transpose(starts, (0, 2, 1)).reshape(-1)
    return idx.reshape(b * e, 1, cap), jnp.transpose(pos, (0, 2, 1)), starts


def kernel(x, c, ctx, c_ctx, ada_w, ada_b, norm_w, w_in, ssd_conv_w, ssd_conv_b, ssd_dt_bias, ssd_a_log, ssd_d,
           ssd_norm_w, hgrn_lb, hgrn_norm_w, w_out, w_router, w_gate, w_up, w_down):
    b, l, d = x.shape
    lc = ctx.shape[1]
    layer = 0
    assert ada_w.shape[0] == 1, "single-layer stack"
    assert l % TM == 0 and lc % Q == 0

    cvec = jnp.zeros((SUBLANES, d), F32).at[:b].set(c).at[b].set(c_ctx)
    mod = _ada(cvec, ada_w[layer], ada_b[layer][None, :])
    ml = mod[:b].reshape(b, 1, 6, d)
    mc = jnp.broadcast_to(mod[b].reshape(1, 1, 6, d), (b, 1, 6, d))
    nw = norm_w[layer]

    wi = w_in[layer]
    o_xbc, o_dt, o_q = SSD_WIDTH, SSD_WIDTH + XBC, SSD_WIDTH + XBC + 2 * SSD_HEADS
    sec = lambda k: wi[:, o_q + k * 512:o_q + (k + 1) * 512]
    w_r = jnp.concatenate([wi[:, o_xbc:o_dt], sec(0), sec(3), sec(1), sec(2), wi[:, :SSD_WIDTH], sec(4),
                           jnp.pad(wi[:, o_dt:o_q], ((0, 0), (0, LANES - 2 * SSD_HEADS)))], axis=1).astype(BF16)
    pad16 = lambda v: jnp.pad(v.reshape(1, 2 * SSD_HEADS), ((0, 0), (0, LANES - 2 * SSD_HEADS)))
    dtb = pad16(ssd_dt_bias[layer])
    alog = pad16(ssd_a_log[layer])
    lbs = jnp.cumsum(jax.nn.softmax(hgrn_lb.astype(F32), axis=0), axis=0)[layer]
    cw, cb = ssd_conv_w[layer], ssd_conv_b[layer][None, :]

    u_ctx = _inproj(ctx, mc[:, :, 0], mc[:, :, 1], nw[0:1], w_r, cw, cb, dtb, lbs[0:1], lbs[1:2], lc)
    u_lat = _inproj(x, ml[:, :, 0], ml[:, :, 1], nw[0:1], w_r, cw, cb, dtb, lbs[0:1], lbs[1:2], GRID_W)

    s0s = jnp.zeros((b, 2, SSD_GROUPS, GW, SSD_STATE), F32)
    s0h = jnp.zeros((b, 2, HGRN_HEADS, HGRN_HEADDIM, HGRN_HEADDIM), F32)
    _, _, scs, sch = _mix(u_ctx, alog, s0s, s0h)
    yf, yb, _, _ = _mix(u_lat, alog, scs, sch)

    dexp = jnp.repeat(ssd_d[layer].astype(F32), SSD_HEADDIM)[None, :]
    x1, h2r, aff = _outproj(x, yf, yb, u_lat, dexp, ssd_norm_w[layer][None, :], hgrn_norm_w[layer][None, :],
                            w_out[layer].astype(BF16), nw[1:2], nw[2:3], ml[:, :, 2], ml[:, :, 3], ml[:, :, 4],
                            w_router[layer].T)

    cap = CAPACITY_FACTOR * l // N_EXPERTS
    assert cap % 256 == 0 and cap >= WIN
    idx, pos_t, starts = _route(aff, cap)
    y = _expert(idx, h2r.reshape(b * l, SUBLANES, LANES), w_gate[layer], w_up[layer], w_down[layer], b, l)
    return _combine(starts, x1, pos_t, jnp.transpose(aff, (0, 2, 1)), y, nw[3:4], ml[:, :, 5])
```

```python
import functools

import jax
import jax.numpy as jnp
from jax import lax
from jax.experimental import pallas as pl
from jax.experimental.pallas import tpu as pltpu

F32 = jnp.float32
BF16 = jnp.bfloat16
EPS = 1e-6

LANES = 128
SUBLANES = 8
BF16_ROWS = 16
VMEM_LIMIT = 56 * 1024 * 1024

GRID_W = 64
SSD_WIDTH, SSD_HEADS, SSD_HEADDIM, SSD_GROUPS, SSD_STATE = 512, 8, 64, 2, 128
HGRN_WIDTH, HGRN_HEADS, HGRN_HEADDIM, HGRN_CHUNK = 512, 4, 128, 64
XBC = SSD_WIDTH + 2 * SSD_GROUPS * SSD_STATE
HG = SSD_HEADS // SSD_GROUPS
GW = HG * SSD_HEADDIM
N_EXPERTS = 16
CAPACITY_FACTOR = 2

A_XBC, A_Q, A_V, A_Z, A_G, A_COLS = 0, 1024, 1536, 2048, 2560, 3072
B_FF, B_FB, B_DT, B_COLS = 0, 512, 1024, 1024 + LANES

Q = 128
TM = 256
TC = 128
WIN = TC + BF16_ROWS
RC = 256
MIX_SAMPLES = 2


def _dot(a, b):
    return jnp.dot(a, b, preferred_element_type=F32)


def _dot_nt(a, b):
    return lax.dot_general(a, b, (((1,), (1,)), ((), ())), preferred_element_type=F32)


def _dot_tn(a, b):
    return _dot(a.T.astype(BF16), b)


def _split2(x):
    x1 = x.astype(BF16)
    return x1, (x - x1.astype(F32)).astype(BF16)


def _dot01(m01, x):
    x1, x2 = _split2(x)
    return _dot(m01, x1) + _dot(m01, x2)


def _dot01r(x, m01):
    x1, x2 = _split2(x)
    return _dot(x1, m01) + _dot(x2, m01)


def _silu(x):
    return x * jax.nn.sigmoid(x)


def _rms(x, w):
    return x * lax.rsqrt(jnp.mean(x * x, axis=-1, keepdims=True) + EPS) * w


def _ada_kernel(c_ref, w_ref, b_ref, o_ref):
    c = c_ref[...]
    o_ref[...] = jnp.dot(_silu(c), w_ref[...], precision=lax.Precision.HIGHEST,
                         preferred_element_type=F32) + b_ref[...]


def _ada(cvec, w, b):
    rows, d = cvec.shape
    n = w.shape[1]
    tn = 1024
    return pl.pallas_call(
        _ada_kernel,
        out_shape=jax.ShapeDtypeStruct((rows, n), F32),
        grid=(n // tn,),
        in_specs=[pl.BlockSpec((rows, d), lambda j: (0, 0)),
                  pl.BlockSpec((d, tn), lambda j: (0, j)),
                  pl.BlockSpec((1, tn), lambda j: (0, j))],
        out_specs=pl.BlockSpec((rows, tn), lambda j: (0, j)),
        compiler_params=pltpu.CompilerParams(dimension_semantics=("arbitrary",), vmem_limit_bytes=VMEM_LIMIT),
        name="ada",
    )(cvec, w, b)


def _inproj_kernel(x_ref, sh_ref, sc_ref, nw_ref, w_ref, cw_ref, cb_ref, dtb_ref, lbf_ref, lbb_ref,
                   a_ref, b_ref, *, row_len):
    x = x_ref[0]
    h = _rms(x, nw_ref[...]) * (1.0 + sc_ref[0]) + sh_ref[0]
    hb = h.astype(BF16)
    tm = x.shape[0]

    xbc = _dot(hb, w_ref[:, A_XBC:A_XBC + XBC])
    pos = lax.broadcasted_iota(jnp.int32, (tm, 1), 0) % row_len
    cw = cw_ref[...]
    acc = cb_ref[...] + cw[2:3, :] * xbc
    for d in (-2, -1, 1, 2):
        shifted = pltpu.roll(xbc, (-d) % tm, 0)
        ok = jnp.logical_and(pos + d >= 0, pos + d < row_len)
        acc = acc + cw[2 + d:3 + d, :] * jnp.where(ok, shifted, 0.0)
    a_ref[0, :, A_XBC:A_XBC + XBC] = _silu(acc).astype(BF16)

    a_ref[0, :, A_Q:A_Q + 512] = _silu(_dot(hb, w_ref[:, A_Q:A_Q + 512])).astype(BF16)
    a_ref[0, :, A_V:A_V + 512] = _dot(hb, w_ref[:, A_V:A_V + 512]).astype(BF16)
    a_ref[0, :, A_Z:A_Z + 512] = _silu(_dot(hb, w_ref[:, A_Z:A_Z + 512])).astype(BF16)
    a_ref[0, :, A_G:A_G + 512] = _silu(_dot(hb, w_ref[:, A_G:A_G + 512])).astype(BF16)
    wb = A_COLS
    lbf = lbf_ref[...]
    b_ref[0, :, B_FF:B_FF + 512] = lbf + (1.0 - lbf) * jax.nn.sigmoid(_dot(hb, w_ref[:, wb + B_FF:wb + B_FF + 512]))
    lbb = lbb_ref[...]
    b_ref[0, :, B_FB:B_FB + 512] = lbb + (1.0 - lbb) * jax.nn.sigmoid(_dot(hb, w_ref[:, wb + B_FB:wb + B_FB + 512]))
    dtr = _dot(hb, w_ref[:, wb + B_DT:wb + B_DT + LANES]) + dtb_ref[...]
    b_ref[0, :, B_DT:B_DT + LANES] = jnp.maximum(dtr, 0.0) + jnp.log(1.0 + jnp.exp(-jnp.abs(dtr)))


def _inproj(x, shift, scale, nw, w_r, cw, cb, dtb, lbf, lbb, row_len):
    b, l, d = x.shape
    tm = min(TM, l)
    vec = lambda n: pl.BlockSpec((1, n), lambda i, j: (0, 0))
    return pl.pallas_call(
        functools.partial(_inproj_kernel, row_len=row_len),
        out_shape=(jax.ShapeDtypeStruct((b, l, A_COLS), BF16), jax.ShapeDtypeStruct((b, l, B_COLS), F32)),
        grid=(b, l // tm),
        in_specs=[pl.BlockSpec((1, tm, d), lambda i, j: (i, j, 0)),
                  pl.BlockSpec((1, 1, d), lambda i, j: (i, 0, 0)),
                  pl.BlockSpec((1, 1, d), lambda i, j: (i, 0, 0)),
                  vec(d),
                  pl.BlockSpec((d, A_COLS + B_COLS), lambda i, j: (0, 0)),
                  pl.BlockSpec((5, XBC), lambda i, j: (0, 0)),
                  vec(XBC), vec(LANES), vec(512), vec(512)],
        out_specs=(pl.BlockSpec((1, tm, A_COLS), lambda i, j: (i, j, 0)),
                   pl.BlockSpec((1, tm, B_COLS), lambda i, j: (i, j, 0))),
        compiler_params=pltpu.CompilerParams(dimension_semantics=("arbitrary", "arbitrary"),
                                             vmem_limit_bytes=VMEM_LIMIT),
        name="inproj",
    )(x, shift, scale, nw, w_r, cw, cb, dtb, lbf, lbb)


def _masks(d):
    row = lax.broadcasted_iota(jnp.int32, (Q, Q), 0)
    col = lax.broadcasted_iota(jnp.int32, (Q, Q), 1)
    tri = (col <= row) if d == 0 else (col >= row)
    blk = jnp.logical_and(tri, (row // HGRN_CHUNK) == (col // HGRN_CHUNK))
    return tri, blk


def _ssd_chain(d, n, xbc_ref, dt_ref, alog_ref, alogx_ref, expm_ref, y_ref, ss_ref):
    tri, _ = _masks(d)
    tri_bf = tri.astype(BF16)
    xbc = xbc_ref[n]
    bm = xbc[:, SSD_WIDTH:SSD_WIDTH + SSD_GROUPS * SSD_STATE]
    cm = xbc[:, SSD_WIDTH + SSD_GROUPS * SSD_STATE:]
    dt = dt_ref[n]
    dt_x = _dot01r(dt, expm_ref[d])
    a_sm = _dot01(tri_bf, dt * -jnp.exp(alog_ref[...]))
    bgs = [bm[:, g * SSD_STATE:(g + 1) * SSD_STATE] for g in range(SSD_GROUPS)]
    cgs = [cm[:, g * SSD_STATE:(g + 1) * SSD_STATE] for g in range(SSD_GROUPS)]
    s_ins = [ss_ref[n, d, g] for g in range(SSD_GROUPS)]
    cbs = [_dot_nt(cgs[g], bgs[g]) for g in range(SSD_GROUPS)]
    y_offs = [_dot_nt(cgs[g], s_ins[g].astype(BF16)) for g in range(SSD_GROUPS)]
    yield
    a_cs = _dot01(tri_bf, dt_x * -jnp.exp(alogx_ref[d:d + 1, :]))
    a_sm_t = a_sm.T
    tot = Q - 1 if d == 0 else 0
    xdt = xbc[:, :SSD_WIDTH].astype(F32) * dt_x
    lane_head = lax.broadcasted_iota(jnp.int32, (Q, GW), 1) // SSD_HEADDIM
    y_ds = []
    for g in range(SSD_GROUPS):
        xdt_g = xdt[:, g * GW:(g + 1) * GW].astype(BF16)
        y_d = None
        for j in range(HG):
            hl = d * SSD_HEADS + g * HG + j
            lmat = jnp.exp(jnp.where(tri, a_sm[:, hl:hl + 1] - a_sm_t[hl:hl + 1, :], -jnp.inf))
            y_full = _dot((cbs[g] * lmat).astype(BF16), xdt_g)
            y_d = y_full if j == 0 else jnp.where(lane_head == j, y_full, y_d)
        y_ds.append(y_d)
        yield
    e_cs = jnp.exp(a_cs)
    xdec = xdt * jnp.exp(a_cs[tot:tot + 1, :] - a_cs)
    sts = [_dot_tn(xdec[:, g * GW:(g + 1) * GW], bgs[g]) for g in range(SSD_GROUPS)]
    for g in range(SSD_GROUPS):
        gs = slice(g * GW, (g + 1) * GW)
        y_ref[n, :, gs] = (y_ds[g] + y_offs[g] * e_cs[:, gs]).astype(y_ref.dtype)
    yield
    for g in range(SSD_GROUPS):
        dtots = [jnp.broadcast_to(jnp.exp(a_sm[tot:tot + 1, hl:hl + 1]), (SSD_HEADDIM, SSD_STATE))
                 for hl in range(d * SSD_HEADS + g * HG, d * SSD_HEADS + (g + 1) * HG)]
        ss_ref[n, d, g] = jnp.concatenate(dtots, axis=0) * s_ins[g] + sts[g]


def _hgrn_chain(d, n, q_ref, v_ref, f_ref, y_ref, sh_ref):
    _, blk = _masks(d)
    ff = f_ref[n]
    bcs = _dot01(blk.astype(BF16), jnp.log(ff))
    vv_bf = v_ref[n]
    heads = [slice(h * HGRN_HEADDIM, (h + 1) * HGRN_HEADDIM) for h in range(HGRN_HEADS)]
    v_ts = [vv_bf[:, sl].astype(F32).T.astype(BF16) for sl in heads]
    yield
    q_dec_bf = (q_ref[n].astype(F32) * jnp.exp(bcs)).astype(BF16)
    kk = 1.0 - ff
    k_inv = (kk * jnp.exp(-bcs)).astype(BF16)
    if d == 0:
        tots = (bcs[HGRN_CHUNK - 1:HGRN_CHUNK, :], bcs[Q - 1:Q, :])
    else:
        tots = (bcs[0:1, :], bcs[HGRN_CHUNK:HGRN_CHUNK + 1, :])
    tot_full = jnp.concatenate([jnp.broadcast_to(tots[0], (HGRN_CHUNK, HGRN_WIDTH)),
                                jnp.broadcast_to(tots[1], (HGRN_CHUNK, HGRN_WIDTH))], axis=0)
    k_end = kk * jnp.exp(tot_full - bcs)
    in_first = lax.broadcasted_iota(jnp.int32, (Q, 1), 0) < HGRN_CHUNK
    k_ends = (jnp.where(in_first, k_end, 0.0).astype(BF16), jnp.where(in_first, 0.0, k_end).astype(BF16))
    c0, c1 = (0, 1) if d == 0 else (1, 0)
    rows = lambda ci: slice(ci * HGRN_CHUNK, (ci + 1) * HGRN_CHUNK)
    s_ts = [sh_ref[n, d, h] for h in range(HGRN_HEADS)]
    atts = [_dot_nt(q_dec_bf[:, sl], k_inv[:, sl]) for sl in heads]
    o_first = [_dot_nt(q_dec_bf[rows(c0), sl], s_ts[h].astype(BF16)) for h, sl in enumerate(heads)]
    upd = [_dot(v_ts[h], k_ends[c0][:, sl]) for h, sl in enumerate(heads)]
    yield
    o_intra = [_dot(jnp.where(blk, atts[h], 0.0).astype(BF16), vv_bf[:, sl]) for h, sl in enumerate(heads)]
    s_ts = [jnp.exp(tots[c0][:, sl]) * s_ts[h] + upd[h] for h, sl in enumerate(heads)]
    o_second = [_dot_nt(q_dec_bf[rows(c1), sl], s_ts[h].astype(BF16)) for h, sl in enumerate(heads)]
    upd = [_dot(v_ts[h], k_ends[c1][:, sl]) for h, sl in enumerate(heads)]
    yield
    for h, sl in enumerate(heads):
        sh_ref[n, d, h] = jnp.exp(tots[c1][:, sl]) * s_ts[h] + upd[h]
        outs = [None, None]
        outs[c0] = o_intra[h][rows(c0)] + o_first[h]
        outs[c1] = o_intra[h][rows(c1)] + o_second[h]
        y_ref[n, :, SSD_WIDTH + h * HGRN_HEADDIM:SSD_WIDTH + (h + 1) * HGRN_HEADDIM] = (
            jnp.concatenate(outs, axis=0).astype(y_ref.dtype))


def _mix_kernel(xbc_f, q_f, v_f, f_f, dt_f, xbc_b, q_b, v_b, f_b, dt_b, alog_ref, alogx_ref, expm_ref,
                s0s_ref, s0h_ref, yf_ref, yb_ref, ss_ref, sh_ref):
    @pl.when(pl.program_id(1) == 0)
    def _():
        ss_ref[...] = s0s_ref[...]
        sh_ref[...] = s0h_ref[...]

    chains = []
    for n in range(yf_ref.shape[0]):
        chains.append(_ssd_chain(0, n, xbc_f, dt_f, alog_ref, alogx_ref, expm_ref, yf_ref, ss_ref))
        chains.append(_hgrn_chain(0, n, q_f, v_f, f_f, yf_ref, sh_ref))
        chains.append(_ssd_chain(1, n, xbc_b, dt_b, alog_ref, alogx_ref, expm_ref, yb_ref, ss_ref))
        chains.append(_hgrn_chain(1, n, q_b, v_b, f_b, yb_ref, sh_ref))
    while chains:
        alive = []
        for ch in chains:
            if next(ch, "done") != "done":
                alive.append(ch)
        chains = alive


def _mix(ua, ub, alog, alogx, expm, s0s, s0h):
    b, l, _ = ua.shape
    nc = l // Q
    ns = MIX_SAMPLES if b % MIX_SAMPLES == 0 else 1
    fwd = lambda w, sec: pl.BlockSpec((ns, Q, w), lambda i, c: (i, c, sec // w))
    bwd = lambda w, sec: pl.BlockSpec((ns, Q, w), lambda i, c: (i, nc - 1 - c, sec // w))
    ss_spec = pl.BlockSpec((ns,) + s0s.shape[1:], lambda i, c: (i, 0, 0, 0, 0))
    sh_spec = pl.BlockSpec((ns,) + s0h.shape[1:], lambda i, c: (i, 0, 0, 0, 0))
    return pl.pallas_call(
        _mix_kernel,
        out_shape=(jax.ShapeDtypeStruct((b, l, 2 * SSD_WIDTH), BF16),
                   jax.ShapeDtypeStruct((b, l, 2 * SSD_WIDTH), BF16),
                   jax.ShapeDtypeStruct(s0s.shape, F32),
                   jax.ShapeDtypeStruct(s0h.shape, F32)),
        grid=(b // ns, nc),
        in_specs=[fwd(XBC, A_XBC), fwd(512, A_Q), fwd(512, A_V), fwd(512, B_FF), fwd(LANES, B_DT),
                  bwd(XBC, A_XBC), bwd(512, A_Q), bwd(512, A_V), bwd(512, B_FB), bwd(LANES, B_DT),
                  pl.BlockSpec((1, LANES), lambda i, c: (0, 0)),
                  pl.BlockSpec((2, SSD_WIDTH), lambda i, c: (0, 0)),
                  pl.BlockSpec((2, LANES, SSD_WIDTH), lambda i, c: (0, 0, 0)),
                  ss_spec, sh_spec],
        out_specs=(pl.BlockSpec((ns, Q, 2 * SSD_WIDTH), lambda i, c: (i, c, 0)),
                   pl.BlockSpec((ns, Q, 2 * SSD_WIDTH), lambda i, c: (i, nc - 1 - c, 0)),
                   ss_spec, sh_spec),
        compiler_params=pltpu.CompilerParams(dimension_semantics=("arbitrary", "arbitrary"),
                                             vmem_limit_bytes=VMEM_LIMIT),
        name="mix",
    )(ua, ua, ua, ub, ub, ua, ua, ua, ub, ub, alog, alogx, expm, s0s, s0h)


def _outproj_kernel(x_ref, yf_ref, yb_ref, xs_ref, z_ref, g_ref, dexp_ref, snw_ref, hnw_ref, wout_ref,
                    nw1_ref, nw2_ref, g1_ref, sh2_ref, sc2_ref, wrt_ref,
                    x1_ref, h2_ref, aff_ref):
    tm = x_ref.shape[1]
    half = tm // 2
    snw = snw_ref[...]
    hnw = hnw_ref[...]

    def half_tile(r0):
        rs = pl.ds(r0, half)
        y = yf_ref[0, rs, :].astype(F32) + yb_ref[0, rs, :].astype(F32)
        ys = (y[:, :SSD_WIDTH] + dexp_ref[...] * xs_ref[0, rs, :].astype(F32)) * z_ref[0, rs, :].astype(F32)
        parts = []
        for g in range(SSD_GROUPS):
            parts.append(_rms(ys[:, g * GW:(g + 1) * GW], snw[:, g * GW:(g + 1) * GW]))
        yh = y[:, SSD_WIDTH:]
        gate = g_ref[0, rs, :].astype(F32)
        for h in range(HGRN_HEADS):
            sl = slice(h * HGRN_HEADDIM, (h + 1) * HGRN_HEADDIM)
            parts.append(_rms(yh[:, sl], hnw[:, sl]) * gate[:, sl])
        ycat = jnp.concatenate(parts, axis=1).astype(BF16)
        proj = _dot(ycat, wout_ref[...])
        yield
        x1 = x_ref[0, rs, :] + g1_ref[0] * _rms(proj, nw1_ref[...])
        x1_ref[0, rs, :] = x1
        h2 = _rms(x1, nw2_ref[...]) * (1.0 + sc2_ref[0]) + sh2_ref[0]
        for s in range(SUBLANES):
            h2_ref[pl.ds(r0 * SUBLANES + s, half, stride=SUBLANES), :] = h2[:, s * LANES:(s + 1) * LANES]
        logits = lax.dot_general(wrt_ref[...], h2, (((1,), (1,)), ((), ())), precision=lax.Precision.HIGHEST,
                                 preferred_element_type=F32)
        yield
        m = jnp.max(logits, axis=0, keepdims=True)
        p = jnp.exp(logits - m)
        aff_ref[0, :, rs] = p / jnp.sum(p, axis=0, keepdims=True)

    chains = [half_tile(0), half_tile(half)]
    while chains:
        chains = [ch for ch in chains if next(ch, "done") != "done"]


def _outproj(x, yf, yb, ua, dexp, snw, hnw, wout, nw1, nw2, g1, sh2, sc2, wrt):
    b, l, d = x.shape
    tm = TM
    nt = l // tm
    vec = lambda n: pl.BlockSpec((1, n), lambda i, j: (0, 0))
    mod = pl.BlockSpec((1, 1, d), lambda i, j: (i, 0, 0))
    usec = lambda sec: pl.BlockSpec((1, tm, 512), lambda i, j: (i, j, sec // 512))
    return pl.pallas_call(
        _outproj_kernel,
        out_shape=(jax.ShapeDtypeStruct((b, l, d), F32),
                   jax.ShapeDtypeStruct((b * l * SUBLANES, LANES), F32),
                   jax.ShapeDtypeStruct((b, N_EXPERTS, l), F32)),
        grid=(b, nt),
        in_specs=[pl.BlockSpec((1, tm, d), lambda i, j: (i, j, 0)),
                  pl.BlockSpec((1, tm, 2 * SSD_WIDTH), lambda i, j: (i, j, 0)),
                  pl.BlockSpec((1, tm, 2 * SSD_WIDTH), lambda i, j: (i, j, 0)),
                  usec(A_XBC), usec(A_Z), usec(A_G),
                  vec(512), vec(512), vec(512),
                  pl.BlockSpec((d, d), lambda i, j: (0, 0)),
                  vec(d), vec(d), mod, mod, mod,
                  pl.BlockSpec((N_EXPERTS, d), lambda i, j: (0, 0))],
        out_specs=(pl.BlockSpec((1, tm, d), lambda i, j: (i, j, 0)),
                   pl.BlockSpec((tm * SUBLANES, LANES), lambda i, j: (i * nt + j, 0)),
                   pl.BlockSpec((1, N_EXPERTS, tm), lambda i, j: (i, 0, j))),
        compiler_params=pltpu.CompilerParams(dimension_semantics=("arbitrary", "arbitrary"),
                                             vmem_limit_bytes=VMEM_LIMIT),
        name="outproj",
    )(x, yf, yb, ua, ua, ua, dexp, snw, hnw, wout, nw1, nw2, g1, sh2, sc2, wrt)


def _expert_kernel(idxc_ref, idxn_ref, h2_hbm, wg_ref, wu_ref, wd_ref, y_ref, xg, wgb, wub, wdb, sem,
                   *, n_tok, cap, nb, ne):
    i = pl.program_id(0)
    j = pl.program_id(1)
    step = i * nb + j
    slot = step % 2
    nxt = jnp.minimum(step + 1, ne * nb - 1)
    jn = lax.rem(nxt, nb)

    @pl.when(j == 0)
    def _():
        wgb[...] = wg_ref[0].astype(BF16)
        wub[...] = wu_ref[0].astype(BF16)
        wdb[...] = wd_ref[0].astype(BF16)

    def row_copy(idx_ref, bb, sl, p):
        tok = bb * n_tok + idx_ref[0, 0, p]
        dst = pl.multiple_of((sl * cap + p) * SUBLANES, SUBLANES)
        return pltpu.make_async_copy(h2_hbm.at[tok], xg.at[pl.ds(dst, SUBLANES), :], sem.at[sl])

    def slot_wait(sl):
        view = xg.at[pl.ds(pl.multiple_of(sl * cap * SUBLANES, SUBLANES), cap * SUBLANES), :]
        pltpu.make_async_copy(view, view, sem.at[sl]).wait()

    @pl.when(step == 0)
    def _():
        def issue(p, carry):
            row_copy(idxc_ref, j, slot, p).start()
            return carry
        lax.fori_loop(0, cap, issue, 0)

    slot_wait(slot)

    def chunk(r, carry):
        for q in range(RC):
            row_copy(idxn_ref, jn, 1 - slot, r * RC + q).start()
        base = (slot * cap + r * RC) * SUBLANES
        xr = jnp.concatenate([xg[pl.ds(base + s, RC, stride=SUBLANES), :] for s in range(SUBLANES)],
                             axis=1).astype(BF16)
        hg = _dot(xr, wgb[...])
        hu = _dot(xr, wub[...])
        act = (_silu(hg) * hu).astype(BF16)
        y_ref[0, 0, pl.ds(pl.multiple_of(r * RC, RC), RC), :] = _dot(act, wdb[...]).astype(BF16)
        return carry

    lax.fori_loop(0, cap // RC, chunk, 0)

    @pl.when(step == ne * nb - 1)
    def _():
        slot_wait(1 - slot)


def _expert(idx, h2r, wg, wu, wd, nb, n_tok):
    ne, d, f = wg.shape
    cap = idx.shape[-1]

    def nxt_row(i, j):
        nxt = jnp.minimum(i * nb + j + 1, ne * nb - 1)
        return (lax.rem(nxt, nb) * ne + lax.div(nxt, nb), 0, 0)

    return pl.pallas_call(
        functools.partial(_expert_kernel, n_tok=n_tok, cap=cap, nb=nb, ne=ne),
        out_shape=jax.ShapeDtypeStruct((nb, ne, cap, d), BF16),
        grid=(ne, nb),
        in_specs=[pl.BlockSpec((1, 1, cap), lambda i, j: (j * ne + i, 0, 0), memory_space=pltpu.SMEM),
                  pl.BlockSpec((1, 1, cap), nxt_row, memory_space=pltpu.SMEM),
                  pl.BlockSpec(memory_space=pl.ANY),
                  pl.BlockSpec((1, d, f), lambda i, j: (i, 0, 0)),
                  pl.BlockSpec((1, d, f), lambda i, j: (i, 0, 0)),
                  pl.BlockSpec((1, f, d), lambda i, j: (i, 0, 0))],
        out_specs=pl.BlockSpec((1, 1, cap, d), lambda i, j: (j, i, 0, 0)),
        scratch_shapes=[pltpu.VMEM((2 * cap * SUBLANES, LANES), F32),
                        pltpu.VMEM((d, f), BF16), pltpu.VMEM((d, f), BF16), pltpu.VMEM((f, d), BF16),
                        pltpu.SemaphoreType.DMA((2,))],
        compiler_params=pltpu.CompilerParams(dimension_semantics=("arbitrary", "arbitrary"),
                                             vmem_limit_bytes=VMEM_LIMIT),
        name="expert",
    )(idx, idx, h2r, wg, wu, wd)


def _combine_kernel(st_ref, x1_ref, pos_ref, aff_ref, y_hbm, nw_ref, g2_ref, o_ref, ywin, sem, *, nt, cap):
    i = pl.program_id(0)
    j = pl.program_id(1)
    step = i * nt + j
    total = pl.num_programs(0) * nt
    slot = step % 2

    def win_start(s, e):
        st = st_ref[s * N_EXPERTS + e]
        st = lax.shift_left(lax.shift_right_logical(st, 4), 4)
        return pl.multiple_of(jnp.minimum(st, cap - WIN), BF16_ROWS)

    def win_copy(bi, s, sl, e):
        return pltpu.make_async_copy(y_hbm.at[bi, e, pl.ds(win_start(s, e), WIN), :], ywin.at[sl, e], sem.at[sl])

    @pl.when(step == 0)
    def _():
        for e in range(N_EXPERTS):
            win_copy(i, step, slot, e).start()

    @pl.when(step + 1 < total)
    def _():
        nxt_b = jnp.where(j + 1 == nt, i + 1, i)
        for e in range(N_EXPERTS):
            win_copy(nxt_b, step + 1, 1 - slot, e).start()

    for e in range(N_EXPERTS):
        win_copy(i, step, slot, e).wait()

    pos = pos_ref[0]
    aff = aff_ref[0]
    lane = lax.broadcasted_iota(jnp.int32, (1, WIN), 1)
    acc = jnp.zeros((TC, o_ref.shape[2]), F32)
    for e in range(N_EXPERTS):
        onehot = (pos[:, e:e + 1] - win_start(step, e) == lane).astype(BF16)
        acc = acc + aff[:, e:e + 1] * _dot(onehot, ywin[slot, e])
    o_ref[0] = x1_ref[0] + g2_ref[0] * _rms(acc, nw_ref[...])


def _combine(starts, x1, pos_t, aff_t, y, nw, g2):
    b, l, d = x1.shape
    nt = l // TC
    cap = y.shape[2]
    grid_spec = pltpu.PrefetchScalarGridSpec(
        num_scalar_prefetch=1,
        grid=(b, nt),
        in_specs=[pl.BlockSpec((1, TC, d), lambda i, j, st: (i, j, 0)),
                  pl.BlockSpec((1, TC, N_EXPERTS), lambda i, j, st: (i, j, 0)),
                  pl.BlockSpec((1, TC, N_EXPERTS), lambda i, j, st: (i, j, 0)),
                  pl.BlockSpec(memory_space=pl.ANY),
                  pl.BlockSpec((1, d), lambda i, j, st: (0, 0)),
                  pl.BlockSpec((1, 1, d), lambda i, j, st: (i, 0, 0))],
        out_specs=pl.BlockSpec((1, TC, d), lambda i, j, st: (i, j, 0)),
        scratch_shapes=[pltpu.VMEM((2, N_EXPERTS, WIN, d), BF16), pltpu.SemaphoreType.DMA((2,))],
    )
    return pl.pallas_call(
        functools.partial(_combine_kernel, nt=nt, cap=cap),
        out_shape=jax.ShapeDtypeStruct((b, l, d), F32),
        grid_spec=grid_spec,
        compiler_params=pltpu.CompilerParams(dimension_semantics=("arbitrary", "arbitrary"),
                                             vmem_limit_bytes=VMEM_LIMIT),
        name="combine",
    )(starts, x1, pos_t, aff_t, y, nw, g2)


def _route(aff, cap):
    b, e, n = aff.shape
    _, idx = lax.top_k(aff, cap)
    idx = jnp.sort(idx, axis=-1).astype(jnp.int32)
    bi = jnp.arange(b)[:, None, None]
    ei = jnp.arange(e)[None, :, None]
    sel = jnp.zeros((b, e, n), jnp.int32).at[bi, ei, idx].set(1)
    incl = jnp.cumsum(sel, axis=-1)
    pos = jnp.where(sel > 0, incl - 1, -(1 << 20))
    starts = jnp.concatenate([jnp.zeros((b, e, 1), jnp.int32), incl[:, :, TC - 1:n - 1:TC]], axis=-1)
    starts = jnp.transpose(starts, (0, 2, 1)).reshape(-1)
    return idx.reshape(b * e, 1, cap), jnp.transpose(pos, (0, 2, 1)), starts


def kernel(x, c, ctx, c_ctx, ada_w, ada_b, norm_w, w_in, ssd_conv_w, ssd_conv_b, ssd_dt_bias, ssd_a_log, ssd_d,
           ssd_norm_w, hgrn_lb, hgrn_norm_w, w_out, w_router, w_gate, w_up, w_down):
    b, l, d = x.shape
    lc = ctx.shape[1]
    layer = 0
    assert ada_w.shape[0] == 1, "single-layer stack"
    assert l % TM == 0 and lc % Q == 0

    cvec = jnp.zeros((SUBLANES, d), F32).at[:b].set(c).at[b].set(c_ctx)
    mod = _ada(cvec, ada_w[layer], ada_b[layer][None, :])
    ml = mod[:b].reshape(b, 1, 6, d)
    mc = jnp.broadcast_to(mod[b].reshape(1, 1, 6, d), (b, 1, 6, d))
    nw = norm_w[layer]

    wi = w_in[layer]
    o_xbc, o_dt, o_q = SSD_WIDTH, SSD_WIDTH + XBC, SSD_WIDTH + XBC + 2 * SSD_HEADS
    sec = lambda k: wi[:, o_q + k * 512:o_q + (k + 1) * 512]
    w_r = jnp.concatenate([wi[:, o_xbc:o_dt], sec(0), sec(3), wi[:, :SSD_WIDTH], sec(4), sec(1), sec(2),
                           jnp.pad(wi[:, o_dt:o_q], ((0, 0), (0, LANES - 2 * SSD_HEADS)))], axis=1).astype(BF16)
    pad16 = lambda v: jnp.pad(v.reshape(1, 2 * SSD_HEADS), ((0, 0), (0, LANES - 2 * SSD_HEADS)))
    dtb = pad16(ssd_dt_bias[layer])
    alog = pad16(ssd_a_log[layer])
    alogx = jnp.repeat(ssd_a_log[layer].astype(F32), SSD_HEADDIM, axis=1)
    lane_id = jnp.arange(LANES)[None, :, None]
    chan_head = (jnp.arange(SSD_WIDTH) // SSD_HEADDIM)[None, None, :]
    expm = (lane_id == chan_head + SSD_HEADS * jnp.arange(2)[:, None, None]).astype(BF16)
    lbs = jnp.cumsum(jax.nn.softmax(hgrn_lb.astype(F32), axis=0), axis=0)[layer]
    cw, cb = ssd_conv_w[layer], ssd_conv_b[layer][None, :]

    ca, cbb = _inproj(ctx, mc[:, :, 0], mc[:, :, 1], nw[0:1], w_r, cw, cb, dtb, lbs[0:1], lbs[1:2], lc)
    ua, ub = _inproj(x, ml[:, :, 0], ml[:, :, 1], nw[0:1], w_r, cw, cb, dtb, lbs[0:1], lbs[1:2], GRID_W)

    s0s = jnp.zeros((b, 2, SSD_GROUPS, GW, SSD_STATE), F32)
    s0h = jnp.zeros((b, 2, HGRN_HEADS, HGRN_HEADDIM, HGRN_HEADDIM), F32)
    _, _, scs, sch = _mix(ca, cbb, alog, alogx, expm, s0s, s0h)
    yf, yb, _, _ = _mix(ua, ub, alog, alogx, expm, scs, sch)

    dexp = jnp.repeat(ssd_d[layer].astype(F32), SSD_HEADDIM)[None, :]
    x1, h2r, aff = _outproj(x, yf, yb, ua, dexp, ssd_norm_w[layer][None, :], hgrn_norm_w[layer][None, :],
                            w_out[layer].astype(BF16), nw[1:2], nw[2:3], ml[:, :, 2], ml[:, :, 3], ml[:, :, 4],
                            w_router[layer].T)

    cap = CAPACITY_FACTOR * l // N_EXPERTS
    assert cap % RC == 0 and cap >= WIN
    idx, pos_t, starts = _route(aff, cap)
    y = _expert(idx, h2r.reshape(b * l, SUBLANES, LANES), w_gate[layer], w_up[layer], w_down[layer], b, l)
    return _combine(starts, x1, pos_t, jnp.transpose(aff, (0, 2, 1)), y, nw[3:4], ml[:, :, 5])
```

```python
import functools

import jax
import jax.numpy as jnp
from jax import lax
from jax.experimental import pallas as pl
from jax.experimental.pallas import tpu as pltpu

F32 = jnp.float32
BF16 = jnp.bfloat16
EPS = 1e-6

LANES = 128
SUBLANES = 8
BF16_ROWS = 16
VMEM_LIMIT = 56 * 1024 * 1024

GRID_W = 64
SSD_WIDTH, SSD_HEADS, SSD_HEADDIM, SSD_GROUPS, SSD_STATE = 512, 8, 64, 2, 128
HGRN_WIDTH, HGRN_HEADS, HGRN_HEADDIM, HGRN_CHUNK = 512, 4, 128, 64
XBC = SSD_WIDTH + 2 * SSD_GROUPS * SSD_STATE
HG = SSD_HEADS // SSD_GROUPS
GW = HG * SSD_HEADDIM
N_EXPERTS = 16
CAPACITY_FACTOR = 2

A_XBC, A_Q, A_V, A_Z, A_G, A_COLS = 0, 1024, 1536, 2048, 2560, 3072
B_FF, B_FB, B_DT, B_COLS = 0, 512, 1024, 1024 + LANES

Q = 128
TM = 256
TM_IN = 256
TC = 128
WIN = TC + BF16_ROWS
WIN_S = 32 + BF16_ROWS
RC = 256
MIX_SAMPLES = 2


def _dot(a, b):
    return jnp.dot(a, b, preferred_element_type=F32)


def _dot_nt(a, b):
    return lax.dot_general(a, b, (((1,), (1,)), ((), ())), preferred_element_type=F32)


def _dot_tn(a, b):
    return _dot(a.T.astype(BF16), b)


def _split2(x):
    x1 = x.astype(BF16)
    return x1, (x - x1.astype(F32)).astype(BF16)


def _dot01(m01, x):
    x1, x2 = _split2(x)
    return _dot(m01, x1) + _dot(m01, x2)


def _dot01r(x, m01):
    x1, x2 = _split2(x)
    return _dot(x1, m01) + _dot(x2, m01)


def _silu(x):
    return x * jax.nn.sigmoid(x)


def _rms(x, w):
    return x * lax.rsqrt(jnp.mean(x * x, axis=-1, keepdims=True) + EPS) * w


def _ada_kernel(c_ref, w_ref, b_ref, o_ref):
    c = c_ref[...]
    o_ref[...] = jnp.dot(_silu(c), w_ref[...], precision=lax.Precision.HIGHEST,
                         preferred_element_type=F32) + b_ref[...]


def _ada(cvec, w, b):
    rows, d = cvec.shape
    n = w.shape[1]
    tn = 1024
    return pl.pallas_call(
        _ada_kernel,
        out_shape=jax.ShapeDtypeStruct((rows, n), F32),
        grid=(n // tn,),
        in_specs=[pl.BlockSpec((rows, d), lambda j: (0, 0)),
                  pl.BlockSpec((d, tn), lambda j: (0, j)),
                  pl.BlockSpec((1, tn), lambda j: (0, j))],
        out_specs=pl.BlockSpec((rows, tn), lambda j: (0, j)),
        compiler_params=pltpu.CompilerParams(dimension_semantics=("arbitrary",), vmem_limit_bytes=VMEM_LIMIT),
        name="ada",
    )(cvec, w, b)


def _inproj_kernel(x_ref, sh_ref, sc_ref, nw_ref, w_ref, cw_ref, cb_ref, dtb_ref, lbf_ref, lbb_ref,
                   a_ref, b_ref, *, row_len):
    x = x_ref[0]
    h = _rms(x, nw_ref[...]) * (1.0 + sc_ref[0]) + sh_ref[0]
    hb = h.astype(BF16)
    tm = x.shape[0]

    def conv_silu(xbc):
        pos = lax.broadcasted_iota(jnp.int32, (tm, 1), 0) % row_len
        cw = cw_ref[...]
        acc = cb_ref[...] + cw[2:3, :] * xbc
        for d in (-2, -1, 1, 2):
            shifted = pltpu.roll(xbc, (-d) % tm, 0)
            ok = jnp.logical_and(pos + d >= 0, pos + d < row_len)
            acc = acc + cw[2 + d:3 + d, :] * jnp.where(ok, shifted, 0.0)
        a_ref[0, :, A_XBC:A_XBC + XBC] = _silu(acc).astype(BF16)

    def put_a(sec, fn):
        def finish(r):
            a_ref[0, :, sec:sec + 512] = fn(r).astype(BF16)
        return finish

    def put_gate(sec, lb_ref):
        def finish(r):
            lb = lb_ref[...]
            b_ref[0, :, sec:sec + 512] = lb + (1.0 - lb) * jax.nn.sigmoid(r)
        return finish

    def put_dt(r):
        dtr = r + dtb_ref[...]
        b_ref[0, :, B_DT:B_DT + LANES] = jnp.maximum(dtr, 0.0) + jnp.log(1.0 + jnp.exp(-jnp.abs(dtr)))

    wb = A_COLS
    sections = [(A_XBC, XBC, conv_silu), (A_Q, 512, put_a(A_Q, _silu)), (A_V, 512, put_a(A_V, lambda r: r)),
                (A_Z, 512, put_a(A_Z, _silu)), (A_G, 512, put_a(A_G, _silu)),
                (wb + B_FF, 512, put_gate(B_FF, lbf_ref)), (wb + B_FB, 512, put_gate(B_FB, lbb_ref)),
                (wb + B_DT, LANES, put_dt)]
    pending = None
    for col, width, finish in sections:
        r = _dot(hb, w_ref[:, col:col + width])
        if pending is not None:
            pending[0](pending[1])
        pending = (finish, r)
    pending[0](pending[1])


def _inproj(x, shift, scale, nw, w_r, cw, cb, dtb, lbf, lbb, row_len):
    b, l, d = x.shape
    tm = min(TM_IN, l)
    assert l % tm == 0 and tm % row_len == 0
    vec = lambda n: pl.BlockSpec((1, n), lambda i, j: (0, 0))
    return pl.pallas_call(
        functools.partial(_inproj_kernel, row_len=row_len),
        out_shape=(jax.ShapeDtypeStruct((b, l, A_COLS), BF16), jax.ShapeDtypeStruct((b, l, B_COLS), F32)),
        grid=(b, l // tm),
        in_specs=[pl.BlockSpec((1, tm, d), lambda i, j: (i, j, 0)),
                  pl.BlockSpec((1, 1, d), lambda i, j: (i, 0, 0)),
                  pl.BlockSpec((1, 1, d), lambda i, j: (i, 0, 0)),
                  vec(d),
                  pl.BlockSpec((d, A_COLS + B_COLS), lambda i, j: (0, 0)),
                  pl.BlockSpec((5, XBC), lambda i, j: (0, 0)),
                  vec(XBC), vec(LANES), vec(512), vec(512)],
        out_specs=(pl.BlockSpec((1, tm, A_COLS), lambda i, j: (i, j, 0)),
                   pl.BlockSpec((1, tm, B_COLS), lambda i, j: (i, j, 0))),
        compiler_params=pltpu.CompilerParams(dimension_semantics=("arbitrary", "arbitrary"),
                                             vmem_limit_bytes=VMEM_LIMIT),
        name="inproj",
    )(x, shift, scale, nw, w_r, cw, cb, dtb, lbf, lbb)


def _masks(d):
    row = lax.broadcasted_iota(jnp.int32, (Q, Q), 0)
    col = lax.broadcasted_iota(jnp.int32, (Q, Q), 1)
    tri = (col <= row) if d == 0 else (col >= row)
    blk = jnp.logical_and(tri, (row // HGRN_CHUNK) == (col // HGRN_CHUNK))
    return tri, blk


def _ssd_chain(d, n, xbc_ref, dt_ref, alog_ref, alogx_ref, expm_ref, y_ref, ss_ref):
    tri, _ = _masks(d)
    tri_bf = tri.astype(BF16)
    xbc = xbc_ref[n]
    bm = xbc[:, SSD_WIDTH:SSD_WIDTH + SSD_GROUPS * SSD_STATE]
    cm = xbc[:, SSD_WIDTH + SSD_GROUPS * SSD_STATE:]
    dt = dt_ref[n]
    dt_x = _dot01r(dt, expm_ref[d])
    a_sm = _dot01(tri_bf, dt * -jnp.exp(alog_ref[...]))
    bgs = [bm[:, g * SSD_STATE:(g + 1) * SSD_STATE] for g in range(SSD_GROUPS)]
    cgs = [cm[:, g * SSD_STATE:(g + 1) * SSD_STATE] for g in range(SSD_GROUPS)]
    s_ins = [ss_ref[n, d, g] for g in range(SSD_GROUPS)]
    cbs = [_dot_nt(cgs[g], bgs[g]) for g in range(SSD_GROUPS)]
    y_offs = [_dot_nt(cgs[g], s_ins[g].astype(BF16)) for g in range(SSD_GROUPS)]
    yield
    a_cs = _dot01(tri_bf, dt_x * -jnp.exp(alogx_ref[d:d + 1, :]))
    a_sm_t = a_sm.T
    tot = Q - 1 if d == 0 else 0
    xdt = xbc[:, :SSD_WIDTH].astype(F32) * dt_x
    lane_head = lax.broadcasted_iota(jnp.int32, (Q, GW), 1) // SSD_HEADDIM
    y_ds = []
    for g in range(SSD_GROUPS):
        xdt_g = xdt[:, g * GW:(g + 1) * GW].astype(BF16)
        y_d = None
        for j in range(HG):
            hl = d * SSD_HEADS + g * HG + j
            lmat = jnp.exp(jnp.where(tri, a_sm[:, hl:hl + 1] - a_sm_t[hl:hl + 1, :], -jnp.inf))
            y_full = _dot((cbs[g] * lmat).astype(BF16), xdt_g)
            y_d = y_full if j == 0 else jnp.where(lane_head == j, y_full, y_d)
        y_ds.append(y_d)
        yield
    e_cs = jnp.exp(a_cs)
    xdec = xdt * jnp.exp(a_cs[tot:tot + 1, :] - a_cs)
    sts = [_dot_tn(xdec[:, g * GW:(g + 1) * GW], bgs[g]) for g in range(SSD_GROUPS)]
    for g in range(SSD_GROUPS):
        gs = slice(g * GW, (g + 1) * GW)
        y_ref[n, :, gs] = (y_ds[g] + y_offs[g] * e_cs[:, gs]).astype(y_ref.dtype)
    yield
    for g in range(SSD_GROUPS):
        dtots = [jnp.broadcast_to(jnp.exp(a_sm[tot:tot + 1, hl:hl + 1]), (SSD_HEADDIM, SSD_STATE))
                 for hl in range(d * SSD_HEADS + g * HG, d * SSD_HEADS + (g + 1) * HG)]
        ss_ref[n, d, g] = jnp.concatenate(dtots, axis=0) * s_ins[g] + sts[g]


def _hgrn_chain(d, n, q_ref, v_ref, f_ref, y_ref, sh_ref):
    _, blk = _masks(d)
    ff = f_ref[n]
    bcs = _dot01(blk.astype(BF16), jnp.log(ff))
    vv_bf = v_ref[n]
    heads = [slice(h * HGRN_HEADDIM, (h + 1) * HGRN_HEADDIM) for h in range(HGRN_HEADS)]
    v_ts = [vv_bf[:, sl].astype(F32).T.astype(BF16) for sl in heads]
    yield
    q_dec_bf = (q_ref[n].astype(F32) * jnp.exp(bcs)).astype(BF16)
    kk = 1.0 - ff
    k_inv = (kk * jnp.exp(-bcs)).astype(BF16)
    if d == 0:
        tots = (bcs[HGRN_CHUNK - 1:HGRN_CHUNK, :], bcs[Q - 1:Q, :])
    else:
        tots = (bcs[0:1, :], bcs[HGRN_CHUNK:HGRN_CHUNK + 1, :])
    tot_full = jnp.concatenate([jnp.broadcast_to(tots[0], (HGRN_CHUNK, HGRN_WIDTH)),
                                jnp.broadcast_to(tots[1], (HGRN_CHUNK, HGRN_WIDTH))], axis=0)
    k_end = kk * jnp.exp(tot_full - bcs)
    in_first = lax.broadcasted_iota(jnp.int32, (Q, 1), 0) < HGRN_CHUNK
    k_ends = (jnp.where(in_first, k_end, 0.0).astype(BF16), jnp.where(in_first, 0.0, k_end).astype(BF16))
    c0, c1 = (0, 1) if d == 0 else (1, 0)
    rows = lambda ci: slice(ci * HGRN_CHUNK, (ci + 1) * HGRN_CHUNK)
    s_ts = [sh_ref[n, d, h] for h in range(HGRN_HEADS)]
    atts = [_dot_nt(q_dec_bf[:, sl], k_inv[:, sl]) for sl in heads]
    o_first = [_dot_nt(q_dec_bf[rows(c0), sl], s_ts[h].astype(BF16)) for h, sl in enumerate(heads)]
    upd = [_dot(v_ts[h], k_ends[c0][:, sl]) for h, sl in enumerate(heads)]
    yield
    o_intra = [_dot(jnp.where(blk, atts[h], 0.0).astype(BF16), vv_bf[:, sl]) for h, sl in enumerate(heads)]
    s_ts = [jnp.exp(tots[c0][:, sl]) * s_ts[h] + upd[h] for h, sl in enumerate(heads)]
    o_second = [_dot_nt(q_dec_bf[rows(c1), sl], s_ts[h].astype(BF16)) for h, sl in enumerate(heads)]
    upd = [_dot(v_ts[h], k_ends[c1][:, sl]) for h, sl in enumerate(heads)]
    yield
    for h, sl in enumerate(heads):
        sh_ref[n, d, h] = jnp.exp(tots[c1][:, sl]) * s_ts[h] + upd[h]
        outs = [None, None]
        outs[c0] = o_intra[h][rows(c0)] + o_first[h]
        outs[c1] = o_intra[h][rows(c1)] + o_second[h]
        y_ref[n, :, SSD_WIDTH + h * HGRN_HEADDIM:SSD_WIDTH + (h + 1) * HGRN_HEADDIM] = (
            jnp.concatenate(outs, axis=0).astype(y_ref.dtype))


def _mix_kernel(xbc_f, q_f, v_f, f_f, dt_f, xbc_b, q_b, v_b, f_b, dt_b, alog_ref, alogx_ref, expm_ref,
                s0s_ref, s0h_ref, yf_ref, yb_ref, ss_ref, sh_ref):
    @pl.when(pl.program_id(1) == 0)
    def _():
        ss_ref[...] = s0s_ref[...]
        sh_ref[...] = s0h_ref[...]

    chains = []
    for n in range(yf_ref.shape[0]):
        chains.append(_ssd_chain(0, n, xbc_f, dt_f, alog_ref, alogx_ref, expm_ref, yf_ref, ss_ref))
        chains.append(_hgrn_chain(0, n, q_f, v_f, f_f, yf_ref, sh_ref))
        chains.append(_ssd_chain(1, n, xbc_b, dt_b, alog_ref, alogx_ref, expm_ref, yb_ref, ss_ref))
        chains.append(_hgrn_chain(1, n, q_b, v_b, f_b, yb_ref, sh_ref))
    while chains:
        alive = []
        for ch in chains:
            if next(ch, "done") != "done":
                alive.append(ch)
        chains = alive


def _mix(ua, ub, alog, alogx, expm, s0s, s0h):
    b, l, _ = ua.shape
    nc = l // Q
    ns = MIX_SAMPLES if b % MIX_SAMPLES == 0 else 1
    fwd = lambda w, sec: pl.BlockSpec((ns, Q, w), lambda i, c: (i, c, sec // w))
    bwd = lambda w, sec: pl.BlockSpec((ns, Q, w), lambda i, c: (i, nc - 1 - c, sec // w))
    ss_spec = pl.BlockSpec((ns,) + s0s.shape[1:], lambda i, c: (i, 0, 0, 0, 0))
    sh_spec = pl.BlockSpec((ns,) + s0h.shape[1:], lambda i, c: (i, 0, 0, 0, 0))
    return pl.pallas_call(
        _mix_kernel,
        out_shape=(jax.ShapeDtypeStruct((b, l, 2 * SSD_WIDTH), BF16),
                   jax.ShapeDtypeStruct((b, l, 2 * SSD_WIDTH), BF16),
                   jax.ShapeDtypeStruct(s0s.shape, F32),
                   jax.ShapeDtypeStruct(s0h.shape, F32)),
        grid=(b // ns, nc),
        in_specs=[fwd(XBC, A_XBC), fwd(512, A_Q), fwd(512, A_V), fwd(512, B_FF), fwd(LANES, B_DT),
                  bwd(XBC, A_XBC), bwd(512, A_Q), bwd(512, A_V), bwd(512, B_FB), bwd(LANES, B_DT),
                  pl.BlockSpec((1, LANES), lambda i, c: (0, 0)),
                  pl.BlockSpec((2, SSD_WIDTH), lambda i, c: (0, 0)),
                  pl.BlockSpec((2, LANES, SSD_WIDTH), lambda i, c: (0, 0, 0)),
                  ss_spec, sh_spec],
        out_specs=(pl.BlockSpec((ns, Q, 2 * SSD_WIDTH), lambda i, c: (i, c, 0)),
                   pl.BlockSpec((ns, Q, 2 * SSD_WIDTH), lambda i, c: (i, nc - 1 - c, 0)),
                   ss_spec, sh_spec),
        compiler_params=pltpu.CompilerParams(dimension_semantics=("arbitrary", "arbitrary"),
                                             vmem_limit_bytes=VMEM_LIMIT),
        name="mix",
    )(ua, ua, ua, ub, ub, ua, ua, ua, ub, ub, alog, alogx, expm, s0s, s0h)


def _outproj_kernel(x_ref, yf_ref, yb_ref, xs_ref, z_ref, g_ref, dexp_ref, snw_ref, hnw_ref, wout_ref,
                    nw1_ref, nw2_ref, g1_ref, sh2_ref, sc2_ref, wrt_ref,
                    x1_ref, h2_ref, aff_ref):
    tm = x_ref.shape[1]
    half = tm // 2
    snw = snw_ref[...]
    hnw = hnw_ref[...]

    def half_tile(r0):
        rs = pl.ds(r0, half)
        y = yf_ref[0, rs, :].astype(F32) + yb_ref[0, rs, :].astype(F32)
        ys = (y[:, :SSD_WIDTH] + dexp_ref[...] * xs_ref[0, rs, :].astype(F32)) * z_ref[0, rs, :].astype(F32)
        parts = []
        for g in range(SSD_GROUPS):
            parts.append(_rms(ys[:, g * GW:(g + 1) * GW], snw[:, g * GW:(g + 1) * GW]))
        yh = y[:, SSD_WIDTH:]
        gate = g_ref[0, rs, :].astype(F32)
        for h in range(HGRN_HEADS):
            sl = slice(h * HGRN_HEADDIM, (h + 1) * HGRN_HEADDIM)
            parts.append(_rms(yh[:, sl], hnw[:, sl]) * gate[:, sl])
        ycat = jnp.concatenate(parts, axis=1).astype(BF16)
        proj = _dot(ycat, wout_ref[...])
        yield
        x1 = x_ref[0, rs, :] + g1_ref[0] * _rms(proj, nw1_ref[...])
        x1_ref[0, rs, :] = x1
        h2 = _rms(x1, nw2_ref[...]) * (1.0 + sc2_ref[0]) + sh2_ref[0]
        for s in range(SUBLANES):
            h2_ref[pl.ds(r0 * SUBLANES + s, half, stride=SUBLANES), :] = h2[:, s * LANES:(s + 1) * LANES]
        logits = lax.dot_general(wrt_ref[...], h2, (((1,), (1,)), ((), ())), precision=lax.Precision.HIGHEST,
                                 preferred_element_type=F32)
        yield
        m = jnp.max(logits, axis=0, keepdims=True)
        p = jnp.exp(logits - m)
        p = p / jnp.sum(p, axis=0, keepdims=True)
        nblk = aff_ref.shape[1] // N_EXPERTS
        blk = pl.program_id(1) * (tm // LANES) + r0 // LANES
        for e in range(N_EXPERTS):
            aff_ref[0, pl.ds(e * nblk + blk, 1), :] = p[e:e + 1, :]

    chains = [half_tile(0), half_tile(half)]
    while chains:
        chains = [ch for ch in chains if next(ch, "done") != "done"]


def _outproj(x, yf, yb, ua, dexp, snw, hnw, wout, nw1, nw2, g1, sh2, sc2, wrt):
    b, l, d = x.shape
    tm = TM
    assert tm == 2 * LANES
    nt = l // tm
    vec = lambda n: pl.BlockSpec((1, n), lambda i, j: (0, 0))
    mod = pl.BlockSpec((1, 1, d), lambda i, j: (i, 0, 0))
    usec = lambda sec: pl.BlockSpec((1, tm, 512), lambda i, j: (i, j, sec // 512))
    return pl.pallas_call(
        _outproj_kernel,
        out_shape=(jax.ShapeDtypeStruct((b, l, d), F32),
                   jax.ShapeDtypeStruct((b * l * SUBLANES, LANES), F32),
                   jax.ShapeDtypeStruct((b, N_EXPERTS * (l // LANES), LANES), F32)),
        grid=(b, nt),
        in_specs=[pl.BlockSpec((1, tm, d), lambda i, j: (i, j, 0)),
                  pl.BlockSpec((1, tm, 2 * SSD_WIDTH), lambda i, j: (i, j, 0)),
                  pl.BlockSpec((1, tm, 2 * SSD_WIDTH), lambda i, j: (i, j, 0)),
                  usec(A_XBC), usec(A_Z), usec(A_G),
                  vec(512), vec(512), vec(512),
                  pl.BlockSpec((d, d), lambda i, j: (0, 0)),
                  vec(d), vec(d), mod, mod, mod,
                  pl.BlockSpec((N_EXPERTS, d), lambda i, j: (0, 0))],
        out_specs=(pl.BlockSpec((1, tm, d), lambda i, j: (i, j, 0)),
                   pl.BlockSpec((tm * SUBLANES, LANES), lambda i, j: (i * nt + j, 0)),
                   pl.BlockSpec((1, N_EXPERTS * (l // LANES), LANES), lambda i, j: (i, 0, 0))),
        compiler_params=pltpu.CompilerParams(dimension_semantics=("arbitrary", "arbitrary"),
                                             vmem_limit_bytes=VMEM_LIMIT),
        name="outproj",
    )(x, yf, yb, ua, ua, ua, dexp, snw, hnw, wout, nw1, nw2, g1, sh2, sc2, wrt)


def _expert_kernel(idxc_ref, idxn_ref, h2_hbm, wg_ref, wu_ref, wd_ref, y_ref, xg, wgb, wub, wdb, sem,
                   *, n_tok, cap, nb, ne):
    i = pl.program_id(0)
    j = pl.program_id(1)
    step = i * nb + j
    slot = step % 2
    nxt = jnp.minimum(step + 1, ne * nb - 1)
    jn = lax.rem(nxt, nb)

    @pl.when(j == 0)
    def _():
        wgb[...] = wg_ref[0].astype(BF16)
        wub[...] = wu_ref[0].astype(BF16)
        wdb[...] = wd_ref[0].astype(BF16)

    def row_copy(idx_ref, bb, sl, p):
        tok = bb * n_tok + idx_ref[0, 0, p]
        dst = pl.multiple_of((sl * cap + p) * SUBLANES, SUBLANES)
        return pltpu.make_async_copy(h2_hbm.at[tok], xg.at[pl.ds(dst, SUBLANES), :], sem.at[sl])

    def slot_wait(sl):
        view = xg.at[pl.ds(pl.multiple_of(sl * cap * SUBLANES, SUBLANES), cap * SUBLANES), :]
        pltpu.make_async_copy(view, view, sem.at[sl]).wait()

    @pl.when(step == 0)
    def _():
        def issue(p, carry):
            row_copy(idxc_ref, j, slot, p).start()
            return carry
        lax.fori_loop(0, cap, issue, 0)

    slot_wait(slot)

    fb = 256
    n_piece = wgb.shape[1] // fb
    per_piece = RC // (2 * n_piece)

    def chunk(r, carry):
        issued = [0]

        def issue_some():
            for q in range(issued[0], issued[0] + per_piece):
                row_copy(idxn_ref, jn, 1 - slot, r * RC + q).start()
            issued[0] += per_piece

        base = (slot * cap + r * RC) * SUBLANES
        xr = jnp.concatenate([xg[pl.ds(base + s, RC, stride=SUBLANES), :] for s in range(SUBLANES)],
                             axis=1).astype(BF16)
        acts = []
        for k in range(n_piece):
            ks = slice(k * fb, (k + 1) * fb)
            acts.append((_silu(_dot(xr, wgb[:, ks])) * _dot(xr, wub[:, ks])).astype(BF16))
            issue_some()
        y = None
        for k in range(n_piece):
            part = _dot(acts[k], wdb[k * fb:(k + 1) * fb, :])
            y = part if y is None else y + part
            issue_some()
        assert issued[0] == RC
        y_ref[0, 0, pl.ds(pl.multiple_of(r * RC, RC), RC), :] = y.astype(BF16)
        return carry

    lax.fori_loop(0, cap // RC, chunk, 0)

    @pl.when(step == ne * nb - 1)
    def _():
        slot_wait(1 - slot)


def _expert(idx, h2r, wg, wu, wd, nb, n_tok):
    ne, d, f = wg.shape
    cap = idx.shape[-1]

    def nxt_row(i, j):
        nxt = jnp.minimum(i * nb + j + 1, ne * nb - 1)
        return (lax.rem(nxt, nb) * ne + lax.div(nxt, nb), 0, 0)

    return pl.pallas_call(
        functools.partial(_expert_kernel, n_tok=n_tok, cap=cap, nb=nb, ne=ne),
        out_shape=jax.ShapeDtypeStruct((nb, ne, cap, d), BF16),
        grid=(ne, nb),
        in_specs=[pl.BlockSpec((1, 1, cap), lambda i, j: (j * ne + i, 0, 0), memory_space=pltpu.SMEM),
                  pl.BlockSpec((1, 1, cap), nxt_row, memory_space=pltpu.SMEM),
                  pl.BlockSpec(memory_space=pl.ANY),
                  pl.BlockSpec((1, d, f), lambda i, j: (i, 0, 0)),
                  pl.BlockSpec((1, d, f), lambda i, j: (i, 0, 0)),
                  pl.BlockSpec((1, f, d), lambda i, j: (i, 0, 0))],
        out_specs=pl.BlockSpec((1, 1, cap, d), lambda i, j: (j, i, 0, 0)),
        scratch_shapes=[pltpu.VMEM((2 * cap * SUBLANES, LANES), F32),
                        pltpu.VMEM((d, f), BF16), pltpu.VMEM((d, f), BF16), pltpu.VMEM((f, d), BF16),
                        pltpu.SemaphoreType.DMA((2,))],
        compiler_params=pltpu.CompilerParams(dimension_semantics=("arbitrary", "arbitrary"),
                                             vmem_limit_bytes=VMEM_LIMIT),
        name="expert",
    )(idx, idx, h2r, wg, wu, wd)


def _combine_kernel(st_ref, x1_ref, pos_ref, aff_ref, y_hbm, nw_ref, g2_ref, o_ref, ywin, sem, *, nt, cap):
    i = pl.program_id(0)
    j = pl.program_id(1)
    step = i * nt + j
    total = pl.num_programs(0) * nt
    slot = step % 2
    last_j = j + 1 == nt
    ni = jnp.where(last_j, i + 1, i)
    nj = jnp.where(last_j, 0, j + 1)

    def span(bi, jj, e):
        base = (bi * (nt + 1) + jj) * N_EXPERTS + e
        st = st_ref[base]
        st16 = lax.shift_left(lax.shift_right_logical(st, 4), 4)
        return st16, st_ref[base + N_EXPERTS] - st16

    def is_small(bi, jj):
        need = span(bi, jj, 0)[1]
        for e in range(1, N_EXPERTS):
            need = jnp.maximum(need, span(bi, jj, e)[1])
        return need <= WIN_S

    def win_start(bi, jj, e, w):
        return pl.multiple_of(jnp.minimum(span(bi, jj, e)[0], cap - w), BF16_ROWS)

    def win_copy(bi, jj, sl, e, w):
        return pltpu.make_async_copy(y_hbm.at[bi, e, pl.ds(win_start(bi, jj, e, w), w), :],
                                     ywin.at[sl, pl.ds(e * w, w), :], sem.at[sl])

    def issue(bi, jj, sl, small):
        for w, cond in ((WIN_S, small), (WIN, jnp.logical_not(small))):
            @pl.when(cond)
            def _():
                for e in range(N_EXPERTS):
                    win_copy(bi, jj, sl, e, w).start()

    small_c = is_small(i, j)

    @pl.when(step == 0)
    def _():
        issue(i, j, slot, small_c)

    @pl.when(step + 1 < total)
    def _():
        issue(ni, nj, 1 - slot, is_small(ni, nj))

    def tile_cols(ref):
        rows = [ref[0, pl.ds(e * nt + j, 1), :].astype(F32) for e in range(N_EXPERTS)]
        rows.append(jnp.zeros((TC - N_EXPERTS, TC), F32))
        return jnp.concatenate(rows, axis=0).T

    pos = tile_cols(pos_ref)
    aff = tile_cols(aff_ref)

    def finish(w):
        for e in range(N_EXPERTS):
            win_copy(i, j, slot, e, w).wait()
        lane = lax.broadcasted_iota(jnp.int32, (1, LANES), 1).astype(F32)
        target = [pos[:, e:e + 1] + (e * w - win_start(i, j, e, w)).astype(F32) for e in range(N_EXPERTS)]
        blocks = []
        for c in range(N_EXPERTS * w // LANES):
            sel = jnp.zeros((TC, LANES), F32)
            for e in range(N_EXPERTS):
                if e * w < (c + 1) * LANES and (e + 1) * w > c * LANES:
                    sel = jnp.where(target[e] - float(c * LANES) == lane, aff[:, e:e + 1], sel)
            blocks.append(sel.astype(BF16))
        acc = None
        kb = 2
        for c in range(0, len(blocks), kb):
            part = _dot(jnp.concatenate(blocks[c:c + kb], axis=1), ywin[slot, c * LANES:(c + kb) * LANES, :])
            acc = part if acc is None else acc + part
        o_ref[0] = x1_ref[0] + g2_ref[0] * _rms(acc, nw_ref[...])

    @pl.when(small_c)
    def _():
        finish(WIN_S)

    @pl.when(jnp.logical_not(small_c))
    def _():
        finish(WIN)


def _combine(starts, x1, pos_st, aff_st, y, nw, g2):
    b, l, d = x1.shape
    assert TC == LANES
    nt = l // TC
    cap = y.shape[2]
    stacked = pl.BlockSpec((1, N_EXPERTS * nt, LANES), lambda i, j, st: (i, 0, 0))
    grid_spec = pltpu.PrefetchScalarGridSpec(
        num_scalar_prefetch=1,
        grid=(b, nt),
        in_specs=[pl.BlockSpec((1, TC, d), lambda i, j, st: (i, j, 0)),
                  stacked, stacked,
                  pl.BlockSpec(memory_space=pl.ANY),
                  pl.BlockSpec((1, d), lambda i, j, st: (0, 0)),
                  pl.BlockSpec((1, 1, d), lambda i, j, st: (i, 0, 0))],
        out_specs=pl.BlockSpec((1, TC, d), lambda i, j, st: (i, j, 0)),
        scratch_shapes=[pltpu.VMEM((2, N_EXPERTS * WIN, d), BF16), pltpu.SemaphoreType.DMA((2,))],
    )
    return pl.pallas_call(
        functools.partial(_combine_kernel, nt=nt, cap=cap),
        out_shape=jax.ShapeDtypeStruct((b, l, d), F32),
        grid_spec=grid_spec,
        compiler_params=pltpu.CompilerParams(dimension_semantics=("arbitrary", "arbitrary"),
                                             vmem_limit_bytes=VMEM_LIMIT),
        name="combine",
    )(starts, x1, pos_st, aff_st, y, nw, g2)


def _route_kernel(aff_ref, pstrict_ref, pos_ref, idx_ref, st_ref, *, cap, nblk):
    rows = N_EXPERTS * nblk
    aff = aff_ref[0]

    def expert_total(x):
        t = jnp.sum(x.reshape(N_EXPERTS, nblk, LANES), axis=(1, 2), keepdims=True)
        return jnp.broadcast_to(t, (N_EXPERTS, nblk, LANES)).reshape(rows, LANES)

    def at_least_cap(cand):
        return expert_total((aff >= cand).astype(F32)) >= cap

    def search_bits(i, thr):
        cand = thr | lax.shift_left(jnp.int32(1), 30 - i)
        return jnp.where(at_least_cap(lax.bitcast_convert_type(cand, F32)), cand, thr)

    thr = lax.fori_loop(0, 31, search_bits, jnp.zeros((rows, LANES), jnp.int32))

    def search_mid(i, lo_hi):
        lo, hi = lo_hi
        mid = 0.5 * (lo + hi)
        ok = at_least_cap(mid)
        return jnp.where(ok, mid, lo), jnp.where(ok, hi, mid)

    lo, hi = lax.fori_loop(0, 24, search_mid,
                           (lax.bitcast_convert_type(thr, F32), lax.bitcast_convert_type(thr + 1, F32)))
    gt = aff >= hi
    eq = jnp.logical_and(aff >= lo, jnp.logical_not(gt))
    need = cap - expert_total(gt.astype(F32))

    r_i = lax.broadcasted_iota(jnp.int32, (LANES, LANES), 0)
    c_i = lax.broadcasted_iota(jnp.int32, (LANES, LANES), 1)
    upper = (r_i <= c_i).astype(BF16)
    ones = jnp.ones((LANES, LANES), BF16)

    def prefix(mask_bf):
        local = _dot(mask_bf, upper)
        tot = _dot(mask_bf, ones)
        return local, tot, _dot(pstrict_ref[...], tot.astype(BF16))

    eq_bf = eq.astype(BF16)
    l_eq, _, o_eq = prefix(eq_bf)
    sel = jnp.logical_or(gt, jnp.logical_and(eq, l_eq + o_eq - eq_bf.astype(F32) < need))
    l_sel, t_sel, o_sel = prefix(sel.astype(BF16))
    pos_ref[0] = jnp.where(sel, (l_sel + o_sel).astype(jnp.int32) - 1, -(1 << 20))
    st_ref[0] = o_sel.astype(jnp.int32)

    p_row = lax.broadcasted_iota(jnp.int32, (1, cap), 1).astype(F32)
    j_col = lax.broadcasted_iota(jnp.int32, (nblk, 1), 0).astype(F32)
    for e in range(N_EXPERTS):
        rs = slice(e * nblk, (e + 1) * nblk)
        off_c = o_sel[rs, 0:1]
        tot_c = t_sel[rs, 0:1]
        inblk = jnp.logical_and(off_c <= p_row, p_row < off_c + tot_c)
        q_row = p_row - jnp.sum(jnp.where(inblk, off_c, 0.0), axis=0, keepdims=True)
        j_row = jnp.sum(jnp.where(inblk, j_col, 0.0), axis=0, keepdims=True)
        l_t = jnp.concatenate([l_sel[rs, :], jnp.zeros((LANES - nblk, LANES), F32)], axis=0).T.astype(BF16)
        in_pad = jnp.concatenate([inblk.astype(BF16), jnp.zeros((LANES - nblk, cap), BF16)], axis=0)
        counts = _dot(l_t, in_pad)
        r_row = jnp.sum((counts <= q_row).astype(F32), axis=0, keepdims=True)
        idx_row = (j_row * LANES + r_row).astype(jnp.int32)
        for a in range(cap // LANES):
            idx_ref[0, e, a:a + 1, :] = idx_row[:, a * LANES:(a + 1) * LANES]


def _route(aff_st, cap):
    b, rows, _ = aff_st.shape
    nblk = rows // N_EXPERTS
    assert nblk <= LANES and cap % LANES == 0
    rho = jnp.arange(rows)
    pstrict = jnp.logical_and(rho[:, None] // nblk == rho[None, :] // nblk, rho[None, :] < rho[:, None]).astype(BF16)
    stacked = pl.BlockSpec((1, rows, LANES), lambda i: (i, 0, 0))
    pos_st, idx, st = pl.pallas_call(
        functools.partial(_route_kernel, cap=cap, nblk=nblk),
        out_shape=(jax.ShapeDtypeStruct((b, rows, LANES), jnp.int32),
                   jax.ShapeDtypeStruct((b, N_EXPERTS, cap // LANES, LANES), jnp.int32),
                   jax.ShapeDtypeStruct((b, rows, LANES), jnp.int32)),
        grid=(b,),
        in_specs=[stacked, pl.BlockSpec((rows, rows), lambda i: (0, 0))],
        out_specs=(stacked, pl.BlockSpec((1, N_EXPERTS, cap // LANES, LANES), lambda i: (i, 0, 0, 0)), stacked),
        compiler_params=pltpu.CompilerParams(dimension_semantics=("arbitrary",), vmem_limit_bytes=VMEM_LIMIT),
        name="route",
    )(aff_st, pstrict)
    starts = jnp.transpose(st[:, :, 0].reshape(b, N_EXPERTS, nblk), (0, 2, 1))
    starts = jnp.concatenate([starts, jnp.full((b, 1, N_EXPERTS), cap, jnp.int32)], axis=1)
    return idx.reshape(b * N_EXPERTS, 1, cap), pos_st, starts.reshape(-1)


def kernel(x, c, ctx, c_ctx, ada_w, ada_b, norm_w, w_in, ssd_conv_w, ssd_conv_b, ssd_dt_bias, ssd_a_log, ssd_d,
           ssd_norm_w, hgrn_lb, hgrn_norm_w, w_out, w_router, w_gate, w_up, w_down):
    b, l, d = x.shape
    lc = ctx.shape[1]
    layer = 0
    assert ada_w.shape[0] == 1, "single-layer stack"
    assert l % TM == 0 and lc % Q == 0

    cvec = jnp.zeros((SUBLANES, d), F32).at[:b].set(c).at[b].set(c_ctx)
    mod = _ada(cvec, ada_w[layer], ada_b[layer][None, :])
    ml = mod[:b].reshape(b, 1, 6, d)
    mc = jnp.broadcast_to(mod[b].reshape(1, 1, 6, d), (b, 1, 6, d))
    nw = norm_w[layer]

    wi = w_in[layer]
    o_xbc, o_dt, o_q = SSD_WIDTH, SSD_WIDTH + XBC, SSD_WIDTH + XBC + 2 * SSD_HEADS
    sec = lambda k: wi[:, o_q + k * 512:o_q + (k + 1) * 512]
    w_r = jnp.concatenate([wi[:, o_xbc:o_dt], sec(0), sec(3), wi[:, :SSD_WIDTH], sec(4), sec(1), sec(2),
                           jnp.pad(wi[:, o_dt:o_q], ((0, 0), (0, LANES - 2 * SSD_HEADS)))], axis=1).astype(BF16)
    pad16 = lambda v: jnp.pad(v.reshape(1, 2 * SSD_HEADS), ((0, 0), (0, LANES - 2 * SSD_HEADS)))
    dtb = pad16(ssd_dt_bias[layer])
    alog = pad16(ssd_a_log[layer])
    alogx = jnp.repeat(ssd_a_log[layer].astype(F32), SSD_HEADDIM, axis=1)
    lane_id = jnp.arange(LANES)[None, :, None]
    chan_head = (jnp.arange(SSD_WIDTH) // SSD_HEADDIM)[None, None, :]
    expm = (lane_id == chan_head + SSD_HEADS * jnp.arange(2)[:, None, None]).astype(BF16)
    lbs = jnp.cumsum(jax.nn.softmax(hgrn_lb.astype(F32), axis=0), axis=0)[layer]
    cw, cb = ssd_conv_w[layer], ssd_conv_b[layer][None, :]

    ca, cbb = _inproj(ctx, mc[:, :, 0], mc[:, :, 1], nw[0:1], w_r, cw, cb, dtb, lbs[0:1], lbs[1:2], lc)
    ua, ub = _inproj(x, ml[:, :, 0], ml[:, :, 1], nw[0:1], w_r, cw, cb, dtb, lbs[0:1], lbs[1:2], GRID_W)

    s0s = jnp.zeros((b, 2, SSD_GROUPS, GW, SSD_STATE), F32)
    s0h = jnp.zeros((b, 2, HGRN_HEADS, HGRN_HEADDIM, HGRN_HEADDIM), F32)
    _, _, scs, sch = _mix(ca, cbb, alog, alogx, expm, s0s, s0h)
    yf, yb, _, _ = _mix(ua, ub, alog, alogx, expm, scs, sch)

    dexp = jnp.repeat(ssd_d[layer].astype(F32), SSD_HEADDIM)[None, :]
    x1, h2r, aff = _outproj(x, yf, yb, ua, dexp, ssd_norm_w[layer][None, :], hgrn_norm_w[layer][None, :],
                            w_out[layer].astype(BF16), nw[1:2], nw[2:3], ml[:, :, 2], ml[:, :, 3], ml[:, :, 4],
                            w_router[layer].T)

    cap = CAPACITY_FACTOR * l // N_EXPERTS
    assert cap % RC == 0 and cap >= WIN
    idx, pos_st, starts = _route(aff, cap)
    y = _expert(idx, h2r.reshape(b * l, SUBLANES, LANES), w_gate[layer], w_up[layer], w_down[layer], b, l)
    return _combine(starts, x1, pos_st, aff, y, nw[3:4], ml[:, :, 5])
```

```python
import functools

import jax
import jax.numpy as jnp
from jax import lax
from jax.experimental import pallas as pl
from jax.experimental.pallas import tpu as pltpu

F32 = jnp.float32
BF16 = jnp.bfloat16
EPS = 1e-6

LANES = 128
SUBLANES = 8
BF16_ROWS = 16
VMEM_LIMIT = 56 * 1024 * 1024

GRID_W = 64
SSD_WIDTH, SSD_HEADS, SSD_HEADDIM, SSD_GROUPS, SSD_STATE = 512, 8, 64, 2, 128
HGRN_WIDTH, HGRN_HEADS, HGRN_HEADDIM, HGRN_CHUNK = 512, 4, 128, 64
XBC = SSD_WIDTH + 2 * SSD_GROUPS * SSD_STATE
HG = SSD_HEADS // SSD_GROUPS
GW = HG * SSD_HEADDIM
N_EXPERTS = 16
CAPACITY_FACTOR = 2

A_XBC, A_Q, A_V, A_Z, A_G, A_COLS = 0, 1024, 1536, 2048, 2560, 3072
B_FF, B_FB, B_DT, B_COLS = 0, 512, 1024, 1024 + LANES

Q = 128
TM = 512
TM_IN = 256
TC = 128
WIN = TC + BF16_ROWS
WIN_S = 32 + BF16_ROWS
RC = 256
MIX_SAMPLES = 4


def _dot(a, b):
    return jnp.dot(a, b, preferred_element_type=F32)


def _dot_nt(a, b):
    return lax.dot_general(a, b, (((1,), (1,)), ((), ())), preferred_element_type=F32)


def _dot_tn(a, b):
    return _dot(a.T.astype(BF16), b)


def _split2(x):
    x1 = x.astype(BF16)
    return x1, (x - x1.astype(F32)).astype(BF16)


def _dot01(m01, x):
    x1, x2 = _split2(x)
    return _dot(m01, x1) + _dot(m01, x2)


def _silu(x):
    return x * jax.nn.sigmoid(x)


def _rms(x, w):
    return x * lax.rsqrt(jnp.mean(x * x, axis=-1, keepdims=True) + EPS) * w


def _ada_kernel(c_ref, w_ref, b_ref, o_ref):
    c = c_ref[...]
    o_ref[...] = jnp.dot(_silu(c), w_ref[...], precision=lax.Precision.HIGHEST,
                         preferred_element_type=F32) + b_ref[...]


def _ada(cvec, w, b):
    rows, d = cvec.shape
    n = w.shape[1]
    tn = 1024
    return pl.pallas_call(
        _ada_kernel,
        out_shape=jax.ShapeDtypeStruct((rows, n), F32),
        grid=(n // tn,),
        in_specs=[pl.BlockSpec((rows, d), lambda j: (0, 0)),
                  pl.BlockSpec((d, tn), lambda j: (0, j)),
                  pl.BlockSpec((1, tn), lambda j: (0, j))],
        out_specs=pl.BlockSpec((rows, tn), lambda j: (0, j)),
        compiler_params=pltpu.CompilerParams(dimension_semantics=("arbitrary",), vmem_limit_bytes=VMEM_LIMIT),
        name="ada",
    )(cvec, w, b)


def _inproj_kernel(x_ref, sh_ref, sc_ref, nw_ref, w_ref, cw_ref, cb_ref, dtb_ref, lbf_ref, lbb_ref,
                   a_ref, b_ref, *, row_len):
    x = x_ref[0]
    h = _rms(x, nw_ref[...]) * (1.0 + sc_ref[0]) + sh_ref[0]
    hb = h.astype(BF16)
    tm = x.shape[0]

    def conv_silu(xbc):
        pos = lax.broadcasted_iota(jnp.int32, (tm, 1), 0) % row_len
        cw = cw_ref[...]
        acc = cb_ref[...] + cw[2:3, :] * xbc
        for d in (-2, -1, 1, 2):
            shifted = pltpu.roll(xbc, (-d) % tm, 0)
            ok = jnp.logical_and(pos + d >= 0, pos + d < row_len)
            acc = acc + cw[2 + d:3 + d, :] * jnp.where(ok, shifted, 0.0)
        a_ref[0, :, A_XBC:A_XBC + XBC] = _silu(acc).astype(BF16)

    def put_a(sec, fn):
        def finish(r):
            a_ref[0, :, sec:sec + 512] = fn(r).astype(BF16)
        return finish

    def put_gate(sec, lb_ref):
        def finish(r):
            lb = lb_ref[...]
            b_ref[0, :, sec:sec + 512] = lb + (1.0 - lb) * jax.nn.sigmoid(r)
        return finish

    def put_dt(r):
        dtr = r + dtb_ref[...]
        b_ref[0, :, B_DT:B_DT + LANES] = jnp.maximum(dtr, 0.0) + jnp.log(1.0 + jnp.exp(-jnp.abs(dtr)))

    wb = A_COLS
    sections = [(A_XBC, XBC, conv_silu), (A_Q, 512, put_a(A_Q, _silu)), (A_V, 512, put_a(A_V, lambda r: r)),
                (A_Z, 512, put_a(A_Z, _silu)), (A_G, 512, put_a(A_G, _silu)),
                (wb + B_FF, 512, put_gate(B_FF, lbf_ref)), (wb + B_FB, 512, put_gate(B_FB, lbb_ref)),
                (wb + B_DT, LANES, put_dt)]
    pending = None
    for col, width, finish in sections:
        r = _dot(hb, w_ref[:, col:col + width])
        if pending is not None:
            pending[0](pending[1])
        pending = (finish, r)
    pending[0](pending[1])


def _inproj(x, shift, scale, nw, w_r, cw, cb, dtb, lbf, lbb, row_len):
    b, l, d = x.shape
    tm = min(TM_IN, l)
    assert l % tm == 0 and tm % row_len == 0
    vec = lambda n: pl.BlockSpec((1, n), lambda i, j: (0, 0))
    return pl.pallas_call(
        functools.partial(_inproj_kernel, row_len=row_len),
        out_shape=(jax.ShapeDtypeStruct((b, l, A_COLS), BF16), jax.ShapeDtypeStruct((b, l, B_COLS), F32)),
        grid=(b, l // tm),
        in_specs=[pl.BlockSpec((1, tm, d), lambda i, j: (i, j, 0)),
                  pl.BlockSpec((1, 1, d), lambda i, j: (i, 0, 0)),
                  pl.BlockSpec((1, 1, d), lambda i, j: (i, 0, 0)),
                  vec(d),
                  pl.BlockSpec((d, A_COLS + B_COLS), lambda i, j: (0, 0)),
                  pl.BlockSpec((5, XBC), lambda i, j: (0, 0)),
                  vec(XBC), vec(LANES), vec(512), vec(512)],
        out_specs=(pl.BlockSpec((1, tm, A_COLS), lambda i, j: (i, j, 0)),
                   pl.BlockSpec((1, tm, B_COLS), lambda i, j: (i, j, 0))),
        compiler_params=pltpu.CompilerParams(dimension_semantics=("arbitrary", "arbitrary"),
                                             vmem_limit_bytes=VMEM_LIMIT),
        name="inproj",
    )(x, shift, scale, nw, w_r, cw, cb, dtb, lbf, lbb)


def _masks(d):
    row = lax.broadcasted_iota(jnp.int32, (Q, Q), 0)
    col = lax.broadcasted_iota(jnp.int32, (Q, Q), 1)
    tri = (col <= row) if d == 0 else (col >= row)
    blk = jnp.logical_and(tri, (row // HGRN_CHUNK) == (col // HGRN_CHUNK))
    return tri, blk


def _ssd_chain(d, n, xbc_ref, dt_ref, alog_ref, expm_ref, y_ref, ss_ref):
    tri, _ = _masks(d)
    tri_bf = tri.astype(BF16)
    xbc = xbc_ref[n]
    bm = xbc[:, SSD_WIDTH:SSD_WIDTH + SSD_GROUPS * SSD_STATE]
    cm = xbc[:, SSD_WIDTH + SSD_GROUPS * SSD_STATE:]
    dt = dt_ref[n]
    spread = lambda v: _dot(v.astype(BF16), expm_ref[d])
    dt_x = spread(dt)
    a_sm = _dot01(tri_bf, dt * -jnp.exp(alog_ref[...]))
    bgs = [bm[:, g * SSD_STATE:(g + 1) * SSD_STATE] for g in range(SSD_GROUPS)]
    cgs = [cm[:, g * SSD_STATE:(g + 1) * SSD_STATE] for g in range(SSD_GROUPS)]
    s_ins = [ss_ref[n, d, g] for g in range(SSD_GROUPS)]
    cbs = [_dot_nt(cgs[g], bgs[g]) for g in range(SSD_GROUPS)]
    y_offs = [_dot_nt(cgs[g], s_ins[g].astype(BF16)) for g in range(SSD_GROUPS)]
    yield
    a_sm_t = a_sm.T
    tot = Q - 1 if d == 0 else 0
    e_cs = spread(jnp.exp(a_sm))
    d_end = spread(jnp.exp(a_sm[tot:tot + 1, :] - a_sm))
    xdt = xbc[:, :SSD_WIDTH].astype(F32) * dt_x
    lane_head = lax.broadcasted_iota(jnp.int32, (Q, GW), 1) // SSD_HEADDIM
    y_ds = []
    for g in range(SSD_GROUPS):
        xdt_g = xdt[:, g * GW:(g + 1) * GW].astype(BF16)
        y_d = None
        for j in range(HG):
            hl = d * SSD_HEADS + g * HG + j
            lmat = jnp.exp(jnp.where(tri, a_sm[:, hl:hl + 1] - a_sm_t[hl:hl + 1, :], -jnp.inf))
            y_full = _dot((cbs[g] * lmat).astype(BF16), xdt_g)
            y_d = y_full if j == 0 else jnp.where(lane_head == j, y_full, y_d)
        y_ds.append(y_d)
        yield
    xdec = xdt * d_end
    sts = [_dot_tn(xdec[:, g * GW:(g + 1) * GW], bgs[g]) for g in range(SSD_GROUPS)]
    for g in range(SSD_GROUPS):
        gs = slice(g * GW, (g + 1) * GW)
        y_ref[n, :, gs] = (y_ds[g] + y_offs[g] * e_cs[:, gs]).astype(y_ref.dtype)
    yield
    for g in range(SSD_GROUPS):
        dtots = [jnp.broadcast_to(jnp.exp(a_sm[tot:tot + 1, hl:hl + 1]), (SSD_HEADDIM, SSD_STATE))
                 for hl in range(d * SSD_HEADS + g * HG, d * SSD_HEADS + (g + 1) * HG)]
        ss_ref[n, d, g] = jnp.concatenate(dtots, axis=0) * s_ins[g] + sts[g]


def _hgrn_chain(d, n, q_ref, v_ref, f_ref, y_ref, sh_ref):
    _, blk = _masks(d)
    ff = f_ref[n]
    bcs = _dot01(blk.astype(BF16), jnp.log(ff))
    vv_bf = v_ref[n]
    heads = [slice(h * HGRN_HEADDIM, (h + 1) * HGRN_HEADDIM) for h in range(HGRN_HEADS)]
    v_ts = [vv_bf[:, sl].astype(F32).T.astype(BF16) for sl in heads]
    yield
    q_dec_bf = (q_ref[n].astype(F32) * jnp.exp(bcs)).astype(BF16)
    kk = 1.0 - ff
    k_inv = (kk * jnp.exp(-bcs)).astype(BF16)
    if d == 0:
        tots = (bcs[HGRN_CHUNK - 1:HGRN_CHUNK, :], bcs[Q - 1:Q, :])
    else:
        tots = (bcs[0:1, :], bcs[HGRN_CHUNK:HGRN_CHUNK + 1, :])
    tot_full = jnp.concatenate([jnp.broadcast_to(tots[0], (HGRN_CHUNK, HGRN_WIDTH)),
                                jnp.broadcast_to(tots[1], (HGRN_CHUNK, HGRN_WIDTH))], axis=0)
    k_end = kk * jnp.exp(tot_full - bcs)
    in_first = lax.broadcasted_iota(jnp.int32, (Q, 1), 0) < HGRN_CHUNK
    k_ends = (jnp.where(in_first, k_end, 0.0).astype(BF16), jnp.where(in_first, 0.0, k_end).astype(BF16))
    c0, c1 = (0, 1) if d == 0 else (1, 0)
    rows = lambda ci: slice(ci * HGRN_CHUNK, (ci + 1) * HGRN_CHUNK)
    s_ts = [sh_ref[n, d, h] for h in range(HGRN_HEADS)]
    atts = [_dot_nt(q_dec_bf[:, sl], k_inv[:, sl]) for sl in heads]
    o_first = [_dot_nt(q_dec_bf[rows(c0), sl], s_ts[h].astype(BF16)) for h, sl in enumerate(heads)]
    upd = [_dot(v_ts[h], k_ends[c0][:, sl]) for h, sl in enumerate(heads)]
    yield
    o_intra = [_dot(jnp.where(blk, atts[h], 0.0).astype(BF16), vv_bf[:, sl]) for h, sl in enumerate(heads)]
    s_ts = [jnp.exp(tots[c0][:, sl]) * s_ts[h] + upd[h] for h, sl in enumerate(heads)]
    o_second = [_dot_nt(q_dec_bf[rows(c1), sl], s_ts[h].astype(BF16)) for h, sl in enumerate(heads)]
    upd = [_dot(v_ts[h], k_ends[c1][:, sl]) for h, sl in enumerate(heads)]
    yield
    for h, sl in enumerate(heads):
        sh_ref[n, d, h] = jnp.exp(tots[c1][:, sl]) * s_ts[h] + upd[h]
        outs = [None, None]
        outs[c0] = o_intra[h][rows(c0)] + o_first[h]
        outs[c1] = o_intra[h][rows(c1)] + o_second[h]
        y_ref[n, :, SSD_WIDTH + h * HGRN_HEADDIM:SSD_WIDTH + (h + 1) * HGRN_HEADDIM] = (
            jnp.concatenate(outs, axis=0).astype(y_ref.dtype))


def _mix_kernel(xbc_f, q_f, v_f, f_f, dt_f, xbc_b, q_b, v_b, f_b, dt_b, alog_ref, expm_ref,
                s0s_ref, s0h_ref, yf_ref, yb_ref, ss_ref, sh_ref):
    @pl.when(pl.program_id(1) == 0)
    def _():
        ss_ref[...] = s0s_ref[...]
        sh_ref[...] = s0h_ref[...]

    chains = []
    for n in range(yf_ref.shape[0]):
        chains.append(_ssd_chain(0, n, xbc_f, dt_f, alog_ref, expm_ref, yf_ref, ss_ref))
        chains.append(_hgrn_chain(0, n, q_f, v_f, f_f, yf_ref, sh_ref))
        chains.append(_ssd_chain(1, n, xbc_b, dt_b, alog_ref, expm_ref, yb_ref, ss_ref))
        chains.append(_hgrn_chain(1, n, q_b, v_b, f_b, yb_ref, sh_ref))
    while chains:
        alive = []
        for ch in chains:
            if next(ch, "done") != "done":
                alive.append(ch)
        chains = alive


def _mix(ua, ub, alog, expm, s0s, s0h):
    b, l, _ = ua.shape
    nc = l // Q
    ns = MIX_SAMPLES if b % MIX_SAMPLES == 0 else 1
    fwd = lambda w, sec: pl.BlockSpec((ns, Q, w), lambda i, c: (i, c, sec // w))
    bwd = lambda w, sec: pl.BlockSpec((ns, Q, w), lambda i, c: (i, nc - 1 - c, sec // w))
    ss_spec = pl.BlockSpec((ns,) + s0s.shape[1:], lambda i, c: (i, 0, 0, 0, 0))
    sh_spec = pl.BlockSpec((ns,) + s0h.shape[1:], lambda i, c: (i, 0, 0, 0, 0))
    return pl.pallas_call(
        _mix_kernel,
        out_shape=(jax.ShapeDtypeStruct((b, l, 2 * SSD_WIDTH), BF16),
                   jax.ShapeDtypeStruct((b, l, 2 * SSD_WIDTH), BF16),
                   jax.ShapeDtypeStruct(s0s.shape, F32),
                   jax.ShapeDtypeStruct(s0h.shape, F32)),
        grid=(b // ns, nc),
        in_specs=[fwd(XBC, A_XBC), fwd(512, A_Q), fwd(512, A_V), fwd(512, B_FF), fwd(LANES, B_DT),
                  bwd(XBC, A_XBC), bwd(512, A_Q), bwd(512, A_V), bwd(512, B_FB), bwd(LANES, B_DT),
                  pl.BlockSpec((1, LANES), lambda i, c: (0, 0)),
                  pl.BlockSpec((2, LANES, SSD_WIDTH), lambda i, c: (0, 0, 0)),
                  ss_spec, sh_spec],
        out_specs=(pl.BlockSpec((ns, Q, 2 * SSD_WIDTH), lambda i, c: (i, c, 0)),
                   pl.BlockSpec((ns, Q, 2 * SSD_WIDTH), lambda i, c: (i, nc - 1 - c, 0)),
                   ss_spec, sh_spec),
        compiler_params=pltpu.CompilerParams(dimension_semantics=("arbitrary", "arbitrary"),
                                             vmem_limit_bytes=VMEM_LIMIT),
        name="mix",
    )(ua, ua, ua, ub, ub, ua, ua, ua, ub, ub, alog, expm, s0s, s0h)


def _outproj_kernel(x_ref, yf_ref, yb_ref, xs_ref, z_ref, g_ref, dexp_ref, snw_ref, hnw_ref, wout_ref,
                    nw1_ref, nw2_ref, g1_ref, sh2_ref, sc2_ref, wrt_ref,
                    x1_ref, h2_ref, aff_ref):
    tm = x_ref.shape[1]
    half = LANES
    snw = snw_ref[...]
    hnw = hnw_ref[...]

    def half_tile(r0):
        rs = pl.ds(r0, half)
        y = yf_ref[0, rs, :].astype(F32) + yb_ref[0, rs, :].astype(F32)
        ys = (y[:, :SSD_WIDTH] + dexp_ref[...] * xs_ref[0, rs, :].astype(F32)) * z_ref[0, rs, :].astype(F32)
        parts = []
        for g in range(SSD_GROUPS):
            parts.append(_rms(ys[:, g * GW:(g + 1) * GW], snw[:, g * GW:(g + 1) * GW]))
        yh = y[:, SSD_WIDTH:]
        gate = g_ref[0, rs, :].astype(F32)
        for h in range(HGRN_HEADS):
            sl = slice(h * HGRN_HEADDIM, (h + 1) * HGRN_HEADDIM)
            parts.append(_rms(yh[:, sl], hnw[:, sl]) * gate[:, sl])
        ycat = jnp.concatenate(parts, axis=1).astype(BF16)
        proj = _dot(ycat, wout_ref[...])
        yield
        x1 = x_ref[0, rs, :] + g1_ref[0] * _rms(proj, nw1_ref[...])
        x1_ref[0, rs, :] = x1
        h2 = _rms(x1, nw2_ref[...]) * (1.0 + sc2_ref[0]) + sh2_ref[0]
        for s in range(SUBLANES):
            h2_ref[pl.ds(r0 * SUBLANES + s, half, stride=SUBLANES), :] = h2[:, s * LANES:(s + 1) * LANES]
        logits = lax.dot_general(wrt_ref[...], h2, (((1,), (1,)), ((), ())), precision=lax.Precision.HIGHEST,
                                 preferred_element_type=F32)
        yield
        m = jnp.max(logits, axis=0, keepdims=True)
        p = jnp.exp(logits - m)
        p = p / jnp.sum(p, axis=0, keepdims=True)
        nblk = aff_ref.shape[1] // N_EXPERTS
        blk = pl.program_id(1) * (tm // LANES) + r0 // LANES
        for e in range(N_EXPERTS):
            aff_ref[0, pl.ds(e * nblk + blk, 1), :] = p[e:e + 1, :]

    chains = [half_tile(r0) for r0 in range(0, tm, half)]
    while chains:
        chains = [ch for ch in chains if next(ch, "done") != "done"]


def _outproj(x, yf, yb, ua, dexp, snw, hnw, wout, nw1, nw2, g1, sh2, sc2, wrt):
    b, l, d = x.shape
    tm = TM
    assert tm % LANES == 0 and l % tm == 0
    nt = l // tm
    vec = lambda n: pl.BlockSpec((1, n), lambda i, j: (0, 0))
    mod = pl.BlockSpec((1, 1, d), lambda i, j: (i, 0, 0))
    usec = lambda sec: pl.BlockSpec((1, tm, 512), lambda i, j: (i, j, sec // 512))
    return pl.pallas_call(
        _outproj_kernel,
        out_shape=(jax.ShapeDtypeStruct((b, l, d), F32),
                   jax.ShapeDtypeStruct((b * l * SUBLANES, LANES), F32),
                   jax.ShapeDtypeStruct((b, N_EXPERTS * (l // LANES), LANES), F32)),
        grid=(b, nt),
        in_specs=[pl.BlockSpec((1, tm, d), lambda i, j: (i, j, 0)),
                  pl.BlockSpec((1, tm, 2 * SSD_WIDTH), lambda i, j: (i, j, 0)),
                  pl.BlockSpec((1, tm, 2 * SSD_WIDTH), lambda i, j: (i, j, 0)),
                  usec(A_XBC), usec(A_Z), usec(A_G),
                  vec(512), vec(512), vec(512),
                  pl.BlockSpec((d, d), lambda i, j: (0, 0)),
                  vec(d), vec(d), mod, mod, mod,
                  pl.BlockSpec((N_EXPERTS, d), lambda i, j: (0, 0))],
        out_specs=(pl.BlockSpec((1, tm, d), lambda i, j: (i, j, 0)),
                   pl.BlockSpec((tm * SUBLANES, LANES), lambda i, j: (i * nt + j, 0)),
                   pl.BlockSpec((1, N_EXPERTS * (l // LANES), LANES), lambda i, j: (i, 0, 0))),
        compiler_params=pltpu.CompilerParams(dimension_semantics=("arbitrary", "arbitrary"),
                                             vmem_limit_bytes=VMEM_LIMIT),
        name="outproj",
    )(x, yf, yb, ua, ua, ua, dexp, snw, hnw, wout, nw1, nw2, g1, sh2, sc2, wrt)


def _expert_kernel(idxc_ref, idxn_ref, h2_hbm, wg_ref, wu_ref, wd_ref, y_ref, xg, wgb, wub, wdb, sem,
                   *, n_tok, cap, nb, ne):
    i = pl.program_id(0)
    j = pl.program_id(1)
    step = i * nb + j
    slot = step % 2
    nxt = jnp.minimum(step + 1, ne * nb - 1)
    jn = lax.rem(nxt, nb)

    @pl.when(j == 0)
    def _():
        wgb[...] = wg_ref[0].astype(BF16)
        wub[...] = wu_ref[0].astype(BF16)
        wdb[...] = wd_ref[0].astype(BF16)

    def row_copy(idx_ref, bb, sl, p):
        tok = bb * n_tok + idx_ref[0, 0, p]
        dst = pl.multiple_of((sl * cap + p) * SUBLANES, SUBLANES)
        return pltpu.make_async_copy(h2_hbm.at[tok], xg.at[pl.ds(dst, SUBLANES), :], sem.at[sl])

    def slot_wait(sl):
        view = xg.at[pl.ds(pl.multiple_of(sl * cap * SUBLANES, SUBLANES), cap * SUBLANES), :]
        pltpu.make_async_copy(view, view, sem.at[sl]).wait()

    @pl.when(step == 0)
    def _():
        def issue(p, carry):
            row_copy(idxc_ref, j, slot, p).start()
            return carry
        lax.fori_loop(0, cap, issue, 0)

    slot_wait(slot)

    fb = 256
    n_piece = wgb.shape[1] // fb
    per_piece = RC // (2 * n_piece)

    def chunk(r, carry):
        issued = [0]

        def issue_some():
            for q in range(issued[0], issued[0] + per_piece):
                row_copy(idxn_ref, jn, 1 - slot, r * RC + q).start()
            issued[0] += per_piece

        base = (slot * cap + r * RC) * SUBLANES
        xr = jnp.concatenate([xg[pl.ds(base + s, RC, stride=SUBLANES), :] for s in range(SUBLANES)],
                             axis=1).astype(BF16)
        acts = []
        for k in range(n_piece):
            ks = slice(k * fb, (k + 1) * fb)
            acts.append((_silu(_dot(xr, wgb[:, ks])) * _dot(xr, wub[:, ks])).astype(BF16))
            issue_some()
        y = None
        for k in range(n_piece):
            part = _dot(acts[k], wdb[k * fb:(k + 1) * fb, :])
            y = part if y is None else y + part
            issue_some()
        assert issued[0] == RC
        y_ref[0, 0, pl.ds(pl.multiple_of(r * RC, RC), RC), :] = y.astype(BF16)
        return carry

    lax.fori_loop(0, cap // RC, chunk, 0)

    @pl.when(step == ne * nb - 1)
    def _():
        slot_wait(1 - slot)


def _expert(idx, h2r, wg, wu, wd, nb, n_tok):
    ne, d, f = wg.shape
    cap = idx.shape[-1]

    def nxt_row(i, j):
        nxt = jnp.minimum(i * nb + j + 1, ne * nb - 1)
        return (lax.rem(nxt, nb) * ne + lax.div(nxt, nb), 0, 0)

    return pl.pallas_call(
        functools.partial(_expert_kernel, n_tok=n_tok, cap=cap, nb=nb, ne=ne),
        out_shape=jax.ShapeDtypeStruct((nb, ne, cap, d), BF16),
        grid=(ne, nb),
        in_specs=[pl.BlockSpec((1, 1, cap), lambda i, j: (j * ne + i, 0, 0), memory_space=pltpu.SMEM),
                  pl.BlockSpec((1, 1, cap), nxt_row, memory_space=pltpu.SMEM),
                  pl.BlockSpec(memory_space=pl.ANY),
                  pl.BlockSpec((1, d, f), lambda i, j: (i, 0, 0)),
                  pl.BlockSpec((1, d, f), lambda i, j: (i, 0, 0)),
                  pl.BlockSpec((1, f, d), lambda i, j: (i, 0, 0))],
        out_specs=pl.BlockSpec((1, 1, cap, d), lambda i, j: (j, i, 0, 0)),
        scratch_shapes=[pltpu.VMEM((2 * cap * SUBLANES, LANES), F32),
                        pltpu.VMEM((d, f), BF16), pltpu.VMEM((d, f), BF16), pltpu.VMEM((f, d), BF16),
                        pltpu.SemaphoreType.DMA((2,))],
        compiler_params=pltpu.CompilerParams(dimension_semantics=("arbitrary", "arbitrary"),
                                             vmem_limit_bytes=VMEM_LIMIT),
        name="expert",
    )(idx, idx, h2r, wg, wu, wd)


def _combine_kernel(st_ref, x1_ref, pos_ref, aff_ref, y_hbm, nw_ref, g2_ref, o_ref, ywin, sem, *, nt, cap):
    i = pl.program_id(0)
    j = pl.program_id(1)
    step = i * nt + j
    total = pl.num_programs(0) * nt
    slot = step % 2
    last_j = j + 1 == nt
    ni = jnp.where(last_j, i + 1, i)
    nj = jnp.where(last_j, 0, j + 1)

    def span(bi, jj, e):
        base = (bi * (nt + 1) + jj) * N_EXPERTS + e
        st = st_ref[base]
        st16 = lax.shift_left(lax.shift_right_logical(st, 4), 4)
        return st16, st_ref[base + N_EXPERTS] - st16

    def is_small(bi, jj):
        need = span(bi, jj, 0)[1]
        for e in range(1, N_EXPERTS):
            need = jnp.maximum(need, span(bi, jj, e)[1])
        return need <= WIN_S

    def win_start(bi, jj, e, w):
        return pl.multiple_of(jnp.minimum(span(bi, jj, e)[0], cap - w), BF16_ROWS)

    def win_copy(bi, jj, sl, e, w):
        return pltpu.make_async_copy(y_hbm.at[bi, e, pl.ds(win_start(bi, jj, e, w), w), :],
                                     ywin.at[sl, pl.ds(e * w, w), :], sem.at[sl])

    def issue(bi, jj, sl, small):
        for w, cond in ((WIN_S, small), (WIN, jnp.logical_not(small))):
            @pl.when(cond)
            def _():
                for e in range(N_EXPERTS):
                    win_copy(bi, jj, sl, e, w).start()

    small_c = is_small(i, j)

    @pl.when(step == 0)
    def _():
        issue(i, j, slot, small_c)

    @pl.when(step + 1 < total)
    def _():
        issue(ni, nj, 1 - slot, is_small(ni, nj))

    pos = [pos_ref[0, pl.ds(e * nt + j, 1), :].astype(F32) for e in range(N_EXPERTS)]
    aff = [aff_ref[0, pl.ds(e * nt + j, 1), :] for e in range(N_EXPERTS)]

    def finish(w):
        for e in range(N_EXPERTS):
            win_copy(i, j, slot, e, w).wait()
        brow = lax.broadcasted_iota(jnp.int32, (LANES, 1), 0).astype(F32)
        target = [pos[e] + (e * w - win_start(i, j, e, w)).astype(F32) for e in range(N_EXPERTS)]
        blocks = []
        for c in range(N_EXPERTS * w // LANES):
            sel_t = jnp.zeros((LANES, TC), F32)
            for e in range(N_EXPERTS):
                if e * w < (c + 1) * LANES and (e + 1) * w > c * LANES:
                    sel_t = jnp.where(target[e] - float(c * LANES) == brow, aff[e], sel_t)
            blocks.append(sel_t.T.astype(BF16))
        acc = None
        kb = 2
        for c in range(0, len(blocks), kb):
            part = _dot(jnp.concatenate(blocks[c:c + kb], axis=1), ywin[slot, c * LANES:(c + kb) * LANES, :])
            acc = part if acc is None else acc + part
        o_ref[0] = x1_ref[0] + g2_ref[0] * _rms(acc, nw_ref[...])

    @pl.when(small_c)
    def _():
        finish(WIN_S)

    @pl.when(jnp.logical_not(small_c))
    def _():
        finish(WIN)


def _combine(starts, x1, pos_st, aff_st, y, nw, g2):
    b, l, d = x1.shape
    assert TC == LANES
    nt = l // TC
    cap = y.shape[2]
    stacked = pl.BlockSpec((1, N_EXPERTS * nt, LANES), lambda i, j, st: (i, 0, 0))
    grid_spec = pltpu.PrefetchScalarGridSpec(
        num_scalar_prefetch=1,
        grid=(b, nt),
        in_specs=[pl.BlockSpec((1, TC, d), lambda i, j, st: (i, j, 0)),
                  stacked, stacked,
                  pl.BlockSpec(memory_space=pl.ANY),
                  pl.BlockSpec((1, d), lambda i, j, st: (0, 0)),
                  pl.BlockSpec((1, 1, d), lambda i, j, st: (i, 0, 0))],
        out_specs=pl.BlockSpec((1, TC, d), lambda i, j, st: (i, j, 0)),
        scratch_shapes=[pltpu.VMEM((2, N_EXPERTS * WIN, d), BF16), pltpu.SemaphoreType.DMA((2,))],
    )
    return pl.pallas_call(
        functools.partial(_combine_kernel, nt=nt, cap=cap),
        out_shape=jax.ShapeDtypeStruct((b, l, d), F32),
        grid_spec=grid_spec,
        compiler_params=pltpu.CompilerParams(dimension_semantics=("arbitrary", "arbitrary"),
                                             vmem_limit_bytes=VMEM_LIMIT),
        name="combine",
    )(starts, x1, pos_st, aff_st, y, nw, g2)


def _route_kernel(aff_ref, pstrict_ref, pos_ref, idx_ref, st_ref, *, cap, nblk):
    rows = N_EXPERTS * nblk
    aff = aff_ref[0]

    def expert_total(x):
        t = jnp.sum(x.reshape(N_EXPERTS, nblk, LANES), axis=(1, 2), keepdims=True)
        return jnp.broadcast_to(t, (N_EXPERTS, nblk, LANES)).reshape(rows, LANES)

    def at_least_cap(cand):
        return expert_total((aff >= cand).astype(F32)) >= cap

    def search_bits(i, thr):
        cand = thr | lax.shift_left(jnp.int32(1), 30 - i)
        return jnp.where(at_least_cap(lax.bitcast_convert_type(cand, F32)), cand, thr)

    thr = lax.fori_loop(0, 31, search_bits, jnp.zeros((rows, LANES), jnp.int32))

    def search_mid(i, lo_hi):
        lo, hi = lo_hi
        mid = 0.5 * (lo + hi)
        ok = at_least_cap(mid)
        return jnp.where(ok, mid, lo), jnp.where(ok, hi, mid)

    lo, hi = lax.fori_loop(0, 24, search_mid,
                           (lax.bitcast_convert_type(thr, F32), lax.bitcast_convert_type(thr + 1, F32)))
    gt = aff >= hi
    eq = jnp.logical_and(aff >= lo, jnp.logical_not(gt))
    need = cap - expert_total(gt.astype(F32))

    r_i = lax.broadcasted_iota(jnp.int32, (LANES, LANES), 0)
    c_i = lax.broadcasted_iota(jnp.int32, (LANES, LANES), 1)
    upper = (r_i <= c_i).astype(BF16)
    ones = jnp.ones((LANES, LANES), BF16)

    def prefix(mask_bf):
        local = _dot(mask_bf, upper)
        tot = _dot(mask_bf, ones)
        return local, tot, _dot(pstrict_ref[...], tot.astype(BF16))

    eq_bf = eq.astype(BF16)
    l_eq, _, o_eq = prefix(eq_bf)
    sel = jnp.logical_or(gt, jnp.logical_and(eq, l_eq + o_eq - eq_bf.astype(F32) < need))
    l_sel, t_sel, o_sel = prefix(sel.astype(BF16))
    pos_ref[0] = jnp.where(sel, (l_sel + o_sel).astype(jnp.int32) - 1, -(1 << 20))
    st_ref[0] = o_sel.astype(jnp.int32)

    p_row = lax.broadcasted_iota(jnp.int32, (1, cap), 1).astype(F32)
    j_col = lax.broadcasted_iota(jnp.int32, (nblk, 1), 0).astype(F32)
    for e in range(N_EXPERTS):
        rs = slice(e * nblk, (e + 1) * nblk)
        off_c = o_sel[rs, 0:1]
        tot_c = t_sel[rs, 0:1]
        inblk = jnp.logical_and(off_c <= p_row, p_row < off_c + tot_c)
        q_row = p_row - jnp.sum(jnp.where(inblk, off_c, 0.0), axis=0, keepdims=True)
        j_row = jnp.sum(jnp.where(inblk, j_col, 0.0), axis=0, keepdims=True)
        l_t = jnp.concatenate([l_sel[rs, :], jnp.zeros((LANES - nblk, LANES), F32)], axis=0).T.astype(BF16)
        in_pad = jnp.concatenate([inblk.astype(BF16), jnp.zeros((LANES - nblk, cap), BF16)], axis=0)
        counts = _dot(l_t, in_pad)
        r_row = jnp.sum((counts <= q_row).astype(F32), axis=0, keepdims=True)
        idx_row = (j_row * LANES + r_row).astype(jnp.int32)
        for a in range(cap // LANES):
            idx_ref[0, e, a:a + 1, :] = idx_row[:, a * LANES:(a + 1) * LANES]


def _route(aff_st, cap):
    b, rows, _ = aff_st.shape
    nblk = rows // N_EXPERTS
    assert nblk <= LANES and cap % LANES == 0
    rho = jnp.arange(rows)
    pstrict = jnp.logical_and(rho[:, None] // nblk == rho[None, :] // nblk, rho[None, :] < rho[:, None]).astype(BF16)
    stacked = pl.BlockSpec((1, rows, LANES), lambda i: (i, 0, 0))
    pos_st, idx, st = pl.pallas_call(
        functools.partial(_route_kernel, cap=cap, nblk=nblk),
        out_shape=(jax.ShapeDtypeStruct((b, rows, LANES), jnp.int32),
                   jax.ShapeDtypeStruct((b, N_EXPERTS, cap // LANES, LANES), jnp.int32),
                   jax.ShapeDtypeStruct((b, rows, LANES), jnp.int32)),
        grid=(b,),
        in_specs=[stacked, pl.BlockSpec((rows, rows), lambda i: (0, 0))],
        out_specs=(stacked, pl.BlockSpec((1, N_EXPERTS, cap // LANES, LANES), lambda i: (i, 0, 0, 0)), stacked),
        compiler_params=pltpu.CompilerParams(dimension_semantics=("arbitrary",), vmem_limit_bytes=VMEM_LIMIT),
        name="route",
    )(aff_st, pstrict)
    starts = jnp.transpose(st[:, :, 0].reshape(b, N_EXPERTS, nblk), (0, 2, 1))
    starts = jnp.concatenate([starts, jnp.full((b, 1, N_EXPERTS), cap, jnp.int32)], axis=1)
    return idx.reshape(b * N_EXPERTS, 1, cap), pos_st, starts.reshape(-1)


def kernel(x, c, ctx, c_ctx, ada_w, ada_b, norm_w, w_in, ssd_conv_w, ssd_conv_b, ssd_dt_bias, ssd_a_log, ssd_d,
           ssd_norm_w, hgrn_lb, hgrn_norm_w, w_out, w_router, w_gate, w_up, w_down):
    b, l, d = x.shape
    lc = ctx.shape[1]
    layer = 0
    assert ada_w.shape[0] == 1, "single-layer stack"
    assert l % TM == 0 and lc % Q == 0

    cvec = jnp.zeros((SUBLANES, d), F32).at[:b].set(c).at[b].set(c_ctx)
    mod = _ada(cvec, ada_w[layer], ada_b[layer][None, :])
    ml = mod[:b].reshape(b, 1, 6, d)
    mc = jnp.broadcast_to(mod[b].reshape(1, 1, 6, d), (b, 1, 6, d))
    nw = norm_w[layer]

    wi = w_in[layer]
    o_xbc, o_dt, o_q = SSD_WIDTH, SSD_WIDTH + XBC, SSD_WIDTH + XBC + 2 * SSD_HEADS
    sec = lambda k: wi[:, o_q + k * 512:o_q + (k + 1) * 512]
    w_r = jnp.concatenate([wi[:, o_xbc:o_dt], sec(0), sec(3), wi[:, :SSD_WIDTH], sec(4), sec(1), sec(2),
                           jnp.pad(wi[:, o_dt:o_q], ((0, 0), (0, LANES - 2 * SSD_HEADS)))], axis=1).astype(BF16)
    pad16 = lambda v: jnp.pad(v.reshape(1, 2 * SSD_HEADS), ((0, 0), (0, LANES - 2 * SSD_HEADS)))
    dtb = pad16(ssd_dt_bias[layer])
    alog = pad16(ssd_a_log[layer])
    lane_id = jnp.arange(LANES)[None, :, None]
    chan_head = (jnp.arange(SSD_WIDTH) // SSD_HEADDIM)[None, None, :]
    expm = (lane_id == chan_head + SSD_HEADS * jnp.arange(2)[:, None, None]).astype(BF16)
    lbs = jnp.cumsum(jax.nn.softmax(hgrn_lb.astype(F32), axis=0), axis=0)[layer]
    cw, cb = ssd_conv_w[layer], ssd_conv_b[layer][None, :]

    ca, cbb = _inproj(ctx, mc[:, :, 0], mc[:, :, 1], nw[0:1], w_r, cw, cb, dtb, lbs[0:1], lbs[1:2], lc)
    ua, ub = _inproj(x, ml[:, :, 0], ml[:, :, 1], nw[0:1], w_r, cw, cb, dtb, lbs[0:1], lbs[1:2], GRID_W)

    s0s = jnp.zeros((b, 2, SSD_GROUPS, GW, SSD_STATE), F32)
    s0h = jnp.zeros((b, 2, HGRN_HEADS, HGRN_HEADDIM, HGRN_HEADDIM), F32)
    _, _, scs, sch = _mix(ca, cbb, alog, expm, s0s, s0h)
    yf, yb, _, _ = _mix(ua, ub, alog, expm, scs, sch)

    dexp = jnp.repeat(ssd_d[layer].astype(F32), SSD_HEADDIM)[None, :]
    x1, h2r, aff = _outproj(x, yf, yb, ua, dexp, ssd_norm_w[layer][None, :], hgrn_norm_w[layer][None, :],
                            w_out[layer].astype(BF16), nw[1:2], nw[2:3], ml[:, :, 2], ml[:, :, 3], ml[:, :, 4],
                            w_router[layer].T)

    cap = CAPACITY_FACTOR * l // N_EXPERTS
    assert cap % RC == 0 and cap >= WIN
    idx, pos_st, starts = _route(aff, cap)
    y = _expert(idx, h2r.reshape(b * l, SUBLANES, LANES), w_gate[layer], w_up[layer], w_down[layer], b, l)
    return _combine(starts, x1, pos_st, aff, y, nw[3:4], ml[:, :, 5])
```

```python
import functools

import jax
import jax.numpy as jnp
from jax import lax
from jax.experimental import pallas as pl
from jax.experimental.pallas import tpu as pltpu

F32 = jnp.float32
BF16 = jnp.bfloat16
EPS = 1e-6

LANES = 128
SUBLANES = 8
BF16_ROWS = 16
VMEM_LIMIT = 56 * 1024 * 1024

GRID_W = 64
SSD_WIDTH, SSD_HEADS, SSD_HEADDIM, SSD_GROUPS, SSD_STATE = 512, 8, 64, 2, 128
HGRN_WIDTH, HGRN_HEADS, HGRN_HEADDIM, HGRN_CHUNK = 512, 4, 128, 64
XBC = SSD_WIDTH + 2 * SSD_GROUPS * SSD_STATE
HG = SSD_HEADS // SSD_GROUPS
GW = HG * SSD_HEADDIM
N_EXPERTS = 16
CAPACITY_FACTOR = 2

A_XBC, A_Q, A_V, A_Z, A_G, A_COLS = 0, 1024, 1536, 2048, 2560, 3072
B_FF, B_FB, B_DT, B_COLS = 0, 512, 1024, 1024 + LANES

Q = 128
TM = 512
TM_IN = 256
TC = 128
WIN = TC + BF16_ROWS
WIN_S = 32 + BF16_ROWS
RC = 256
N_SLOTS = 3
MIX_SAMPLES = 4


def _dot(a, b):
    return jnp.dot(a, b, preferred_element_type=F32)


def _dot_nt(a, b):
    return lax.dot_general(a, b, (((1,), (1,)), ((), ())), preferred_element_type=F32)


def _dot_tn(a, b):
    return _dot(a.T.astype(BF16), b)


def _split2(x):
    x1 = x.astype(BF16)
    return x1, (x - x1.astype(F32)).astype(BF16)


def _dot01(m01, x):
    x1, x2 = _split2(x)
    return _dot(m01, x1) + _dot(m01, x2)


def _silu(x):
    return x * jax.nn.sigmoid(x)


def _rms(x, w):
    return x * lax.rsqrt(jnp.mean(x * x, axis=-1, keepdims=True) + EPS) * w


def _ada_kernel(c_ref, w_ref, b_ref, o_ref):
    c = c_ref[...]
    o_ref[...] = jnp.dot(_silu(c), w_ref[...], precision=lax.Precision.HIGHEST,
                         preferred_element_type=F32) + b_ref[...]


def _ada(cvec, w, b):
    rows, d = cvec.shape
    n = w.shape[1]
    tn = 1024
    return pl.pallas_call(
        _ada_kernel,
        out_shape=jax.ShapeDtypeStruct((rows, n), F32),
        grid=(n // tn,),
        in_specs=[pl.BlockSpec((rows, d), lambda j: (0, 0)),
                  pl.BlockSpec((d, tn), lambda j: (0, j)),
                  pl.BlockSpec((1, tn), lambda j: (0, j))],
        out_specs=pl.BlockSpec((rows, tn), lambda j: (0, j)),
        compiler_params=pltpu.CompilerParams(dimension_semantics=("arbitrary",), vmem_limit_bytes=VMEM_LIMIT),
        name="ada",
    )(cvec, w, b)


def _inproj_kernel(x_ref, sh_ref, sc_ref, nw_ref, w_ref, cw_ref, cb_ref, dtb_ref, lbf_ref, lbb_ref,
                   a_ref, b_ref, *, row_len):
    x = x_ref[0]
    h = _rms(x, nw_ref[...]) * (1.0 + sc_ref[0]) + sh_ref[0]
    hb = h.astype(BF16)
    tm = x.shape[0]

    def conv_silu(xbc):
        pos = lax.broadcasted_iota(jnp.int32, (tm, 1), 0) % row_len
        cw = cw_ref[...]
        acc = cb_ref[...] + cw[2:3, :] * xbc
        for d in (-2, -1, 1, 2):
            shifted = pltpu.roll(xbc, (-d) % tm, 0)
            ok = jnp.logical_and(pos + d >= 0, pos + d < row_len)
            acc = acc + cw[2 + d:3 + d, :] * jnp.where(ok, shifted, 0.0)
        a_ref[0, :, A_XBC:A_XBC + XBC] = _silu(acc).astype(BF16)

    def put_a(sec, fn):
        def finish(r):
            a_ref[0, :, sec:sec + 512] = fn(r).astype(BF16)
        return finish

    def put_gate(sec, lb_ref):
        def finish(r):
            lb = lb_ref[...]
            b_ref[0, :, sec:sec + 512] = lb + (1.0 - lb) * jax.nn.sigmoid(r)
        return finish

    def put_dt(r):
        dtr = r + dtb_ref[...]
        b_ref[0, :, B_DT:B_DT + LANES] = jnp.maximum(dtr, 0.0) + jnp.log(1.0 + jnp.exp(-jnp.abs(dtr)))

    wb = A_COLS
    sections = [(A_XBC, XBC, conv_silu), (A_Q, 512, put_a(A_Q, _silu)), (A_V, 512, put_a(A_V, lambda r: r)),
                (A_Z, 512, put_a(A_Z, _silu)), (A_G, 512, put_a(A_G, _silu)),
                (wb + B_FF, 512, put_gate(B_FF, lbf_ref)), (wb + B_FB, 512, put_gate(B_FB, lbb_ref)),
                (wb + B_DT, LANES, put_dt)]
    pending = None
    for col, width, finish in sections:
        r = _dot(hb, w_ref[:, col:col + width])
        if pending is not None:
            pending[0](pending[1])
        pending = (finish, r)
    pending[0](pending[1])


def _inproj(x, shift, scale, nw, w_r, cw, cb, dtb, lbf, lbb, row_len):
    b, l, d = x.shape
    tm = min(TM_IN, l)
    assert l % tm == 0 and tm % row_len == 0
    vec = lambda n: pl.BlockSpec((1, n), lambda i, j: (0, 0))
    return pl.pallas_call(
        functools.partial(_inproj_kernel, row_len=row_len),
        out_shape=(jax.ShapeDtypeStruct((b, l, A_COLS), BF16), jax.ShapeDtypeStruct((b, l, B_COLS), F32)),
        grid=(b, l // tm),
        in_specs=[pl.BlockSpec((1, tm, d), lambda i, j: (i, j, 0)),
                  pl.BlockSpec((1, 1, d), lambda i, j: (i, 0, 0)),
                  pl.BlockSpec((1, 1, d), lambda i, j: (i, 0, 0)),
                  vec(d),
                  pl.BlockSpec((d, A_COLS + B_COLS), lambda i, j: (0, 0)),
                  pl.BlockSpec((5, XBC), lambda i, j: (0, 0)),
                  vec(XBC), vec(LANES), vec(512), vec(512)],
        out_specs=(pl.BlockSpec((1, tm, A_COLS), lambda i, j: (i, j, 0)),
                   pl.BlockSpec((1, tm, B_COLS), lambda i, j: (i, j, 0))),
        compiler_params=pltpu.CompilerParams(dimension_semantics=("arbitrary", "arbitrary"),
                                             vmem_limit_bytes=VMEM_LIMIT),
        name="inproj",
    )(x, shift, scale, nw, w_r, cw, cb, dtb, lbf, lbb)


def _masks(d):
    row = lax.broadcasted_iota(jnp.int32, (Q, Q), 0)
    col = lax.broadcasted_iota(jnp.int32, (Q, Q), 1)
    tri = (col <= row) if d == 0 else (col >= row)
    blk = jnp.logical_and(tri, (row // HGRN_CHUNK) == (col // HGRN_CHUNK))
    return tri, blk


def _ssd_chain(d, n, xbc_ref, dt_ref, alog_ref, expm_ref, y_ref, ss_ref):
    tri, _ = _masks(d)
    tri_bf = tri.astype(BF16)
    xbc = xbc_ref[n]
    bm = xbc[:, SSD_WIDTH:SSD_WIDTH + SSD_GROUPS * SSD_STATE]
    cm = xbc[:, SSD_WIDTH + SSD_GROUPS * SSD_STATE:]
    dt = dt_ref[n]
    spread = lambda v: _dot(v.astype(BF16), expm_ref[d])
    dt_x = spread(dt)
    a_sm = _dot01(tri_bf, dt * -jnp.exp(alog_ref[...]))
    bgs = [bm[:, g * SSD_STATE:(g + 1) * SSD_STATE] for g in range(SSD_GROUPS)]
    cgs = [cm[:, g * SSD_STATE:(g + 1) * SSD_STATE] for g in range(SSD_GROUPS)]
    s_ins = [ss_ref[n, d, g] for g in range(SSD_GROUPS)]
    cbs = [_dot_nt(cgs[g], bgs[g]) for g in range(SSD_GROUPS)]
    y_offs = [_dot_nt(cgs[g], s_ins[g].astype(BF16)) for g in range(SSD_GROUPS)]
    yield
    a_sm_t = a_sm.T
    tot = Q - 1 if d == 0 else 0
    e_cs = spread(jnp.exp(a_sm))
    d_end = spread(jnp.exp(a_sm[tot:tot + 1, :] - a_sm))
    xdt = xbc[:, :SSD_WIDTH].astype(F32) * dt_x
    lane_head = lax.broadcasted_iota(jnp.int32, (Q, GW), 1) // SSD_HEADDIM
    y_ds = []
    for g in range(SSD_GROUPS):
        xdt_g = xdt[:, g * GW:(g + 1) * GW].astype(BF16)
        y_d = None
        for j in range(HG):
            hl = d * SSD_HEADS + g * HG + j
            lmat = jnp.exp(jnp.where(tri, a_sm[:, hl:hl + 1] - a_sm_t[hl:hl + 1, :], -jnp.inf))
            y_full = _dot((cbs[g] * lmat).astype(BF16), xdt_g)
            y_d = y_full if j == 0 else jnp.where(lane_head == j, y_full, y_d)
        y_ds.append(y_d)
        yield
    xdec = xdt * d_end
    sts = [_dot_tn(xdec[:, g * GW:(g + 1) * GW], bgs[g]) for g in range(SSD_GROUPS)]
    for g in range(SSD_GROUPS):
        gs = slice(g * GW, (g + 1) * GW)
        y_ref[n, :, gs] = (y_ds[g] + y_offs[g] * e_cs[:, gs]).astype(y_ref.dtype)
    yield
    for g in range(SSD_GROUPS):
        dtots = [jnp.broadcast_to(jnp.exp(a_sm[tot:tot + 1, hl:hl + 1]), (SSD_HEADDIM, SSD_STATE))
                 for hl in range(d * SSD_HEADS + g * HG, d * SSD_HEADS + (g + 1) * HG)]
        ss_ref[n, d, g] = jnp.concatenate(dtots, axis=0) * s_ins[g] + sts[g]


def _hgrn_chain(d, n, q_ref, v_ref, f_ref, y_ref, sh_ref):
    _, blk = _masks(d)
    ff = f_ref[n]
    bcs = _dot01(blk.astype(BF16), jnp.log(ff))
    vv_bf = v_ref[n]
    heads = [slice(h * HGRN_HEADDIM, (h + 1) * HGRN_HEADDIM) for h in range(HGRN_HEADS)]
    v_ts = [vv_bf[:, sl].astype(F32).T.astype(BF16) for sl in heads]
    yield
    q_dec_bf = (q_ref[n].astype(F32) * jnp.exp(bcs)).astype(BF16)
    kk = 1.0 - ff
    k_inv = (kk * jnp.exp(-bcs)).astype(BF16)
    if d == 0:
        tots = (bcs[HGRN_CHUNK - 1:HGRN_CHUNK, :], bcs[Q - 1:Q, :])
    else:
        tots = (bcs[0:1, :], bcs[HGRN_CHUNK:HGRN_CHUNK + 1, :])
    tot_full = jnp.concatenate([jnp.broadcast_to(tots[0], (HGRN_CHUNK, HGRN_WIDTH)),
                                jnp.broadcast_to(tots[1], (HGRN_CHUNK, HGRN_WIDTH))], axis=0)
    k_end = kk * jnp.exp(tot_full - bcs)
    in_first = lax.broadcasted_iota(jnp.int32, (Q, 1), 0) < HGRN_CHUNK
    k_ends = (jnp.where(in_first, k_end, 0.0).astype(BF16), jnp.where(in_first, 0.0, k_end).astype(BF16))
    c0, c1 = (0, 1) if d == 0 else (1, 0)
    rows = lambda ci: slice(ci * HGRN_CHUNK, (ci + 1) * HGRN_CHUNK)
    s_ts = [sh_ref[n, d, h] for h in range(HGRN_HEADS)]
    atts = [_dot_nt(q_dec_bf[:, sl], k_inv[:, sl]) for sl in heads]
    o_first = [_dot_nt(q_dec_bf[rows(c0), sl], s_ts[h].astype(BF16)) for h, sl in enumerate(heads)]
    upd = [_dot(v_ts[h], k_ends[c0][:, sl]) for h, sl in enumerate(heads)]
    yield
    o_intra = [_dot(jnp.where(blk, atts[h], 0.0).astype(BF16), vv_bf[:, sl]) for h, sl in enumerate(heads)]
    s_ts = [jnp.exp(tots[c0][:, sl]) * s_ts[h] + upd[h] for h, sl in enumerate(heads)]
    o_second = [_dot_nt(q_dec_bf[rows(c1), sl], s_ts[h].astype(BF16)) for h, sl in enumerate(heads)]
    upd = [_dot(v_ts[h], k_ends[c1][:, sl]) for h, sl in enumerate(heads)]
    yield
    for h, sl in enumerate(heads):
        sh_ref[n, d, h] = jnp.exp(tots[c1][:, sl]) * s_ts[h] + upd[h]
        outs = [None, None]
        outs[c0] = o_intra[h][rows(c0)] + o_first[h]
        outs[c1] = o_intra[h][rows(c1)] + o_second[h]
        y_ref[n, :, SSD_WIDTH + h * HGRN_HEADDIM:SSD_WIDTH + (h + 1) * HGRN_HEADDIM] = (
            jnp.concatenate(outs, axis=0).astype(y_ref.dtype))


def _mix_kernel(xbc_f, q_f, v_f, f_f, dt_f, xbc_b, q_b, v_b, f_b, dt_b, alog_ref, expm_ref,
                s0s_ref, s0h_ref, yf_ref, yb_ref, ss_ref, sh_ref):
    @pl.when(pl.program_id(1) == 0)
    def _():
        ss_ref[...] = s0s_ref[...]
        sh_ref[...] = s0h_ref[...]

    chains = []
    for n in range(yf_ref.shape[0]):
        chains.append(_ssd_chain(0, n, xbc_f, dt_f, alog_ref, expm_ref, yf_ref, ss_ref))
        chains.append(_hgrn_chain(0, n, q_f, v_f, f_f, yf_ref, sh_ref))
        chains.append(_ssd_chain(1, n, xbc_b, dt_b, alog_ref, expm_ref, yb_ref, ss_ref))
        chains.append(_hgrn_chain(1, n, q_b, v_b, f_b, yb_ref, sh_ref))
    while chains:
        alive = []
        for ch in chains:
            if next(ch, "done") != "done":
                alive.append(ch)
        chains = alive


def _mix(ua, ub, alog, expm, s0s, s0h):
    b, l, _ = ua.shape
    nc = l // Q
    ns = MIX_SAMPLES if b % MIX_SAMPLES == 0 else 1
    fwd = lambda w, sec: pl.BlockSpec((ns, Q, w), lambda i, c: (i, c, sec // w))
    bwd = lambda w, sec: pl.BlockSpec((ns, Q, w), lambda i, c: (i, nc - 1 - c, sec // w))
    ss_spec = pl.BlockSpec((ns,) + s0s.shape[1:], lambda i, c: (i, 0, 0, 0, 0))
    sh_spec = pl.BlockSpec((ns,) + s0h.shape[1:], lambda i, c: (i, 0, 0, 0, 0))
    return pl.pallas_call(
        _mix_kernel,
        out_shape=(jax.ShapeDtypeStruct((b, l, 2 * SSD_WIDTH), BF16),
                   jax.ShapeDtypeStruct((b, l, 2 * SSD_WIDTH), BF16),
                   jax.ShapeDtypeStruct(s0s.shape, F32),
                   jax.ShapeDtypeStruct(s0h.shape, F32)),
        grid=(b // ns, nc),
        in_specs=[fwd(XBC, A_XBC), fwd(512, A_Q), fwd(512, A_V), fwd(512, B_FF), fwd(LANES, B_DT),
                  bwd(XBC, A_XBC), bwd(512, A_Q), bwd(512, A_V), bwd(512, B_FB), bwd(LANES, B_DT),
                  pl.BlockSpec((1, LANES), lambda i, c: (0, 0)),
                  pl.BlockSpec((2, LANES, SSD_WIDTH), lambda i, c: (0, 0, 0)),
                  ss_spec, sh_spec],
        out_specs=(pl.BlockSpec((ns, Q, 2 * SSD_WIDTH), lambda i, c: (i, c, 0)),
                   pl.BlockSpec((ns, Q, 2 * SSD_WIDTH), lambda i, c: (i, nc - 1 - c, 0)),
                   ss_spec, sh_spec),
        compiler_params=pltpu.CompilerParams(dimension_semantics=("arbitrary", "arbitrary"),
                                             vmem_limit_bytes=VMEM_LIMIT),
        name="mix",
    )(ua, ua, ua, ub, ub, ua, ua, ua, ub, ub, alog, expm, s0s, s0h)


def _outproj_kernel(x_ref, yf_ref, yb_ref, xs_ref, z_ref, g_ref, dexp_ref, snw_ref, hnw_ref, wout_ref,
                    nw1_ref, nw2_ref, g1_ref, sh2_ref, sc2_ref, wrt_ref,
                    x1_ref, h2_ref, aff_ref):
    tm = x_ref.shape[1]
    half = LANES
    snw = snw_ref[...]
    hnw = hnw_ref[...]

    def half_tile(r0):
        rs = pl.ds(r0, half)
        y = yf_ref[0, rs, :].astype(F32) + yb_ref[0, rs, :].astype(F32)
        ys = (y[:, :SSD_WIDTH] + dexp_ref[...] * xs_ref[0, rs, :].astype(F32)) * z_ref[0, rs, :].astype(F32)
        parts = []
        for g in range(SSD_GROUPS):
            parts.append(_rms(ys[:, g * GW:(g + 1) * GW], snw[:, g * GW:(g + 1) * GW]))
        yh = y[:, SSD_WIDTH:]
        gate = g_ref[0, rs, :].astype(F32)
        for h in range(HGRN_HEADS):
            sl = slice(h * HGRN_HEADDIM, (h + 1) * HGRN_HEADDIM)
            parts.append(_rms(yh[:, sl], hnw[:, sl]) * gate[:, sl])
        ycat = jnp.concatenate(parts, axis=1).astype(BF16)
        proj = _dot(ycat, wout_ref[...])
        yield
        x1 = x_ref[0, rs, :] + g1_ref[0] * _rms(proj, nw1_ref[...])
        x1_ref[0, rs, :] = x1
        h2 = _rms(x1, nw2_ref[...]) * (1.0 + sc2_ref[0]) + sh2_ref[0]
        for s in range(SUBLANES):
            h2_ref[pl.ds(r0 * SUBLANES + s, half, stride=SUBLANES), :] = h2[:, s * LANES:(s + 1) * LANES]
        logits = lax.dot_general(wrt_ref[...], h2, (((1,), (1,)), ((), ())), precision=lax.Precision.HIGHEST,
                                 preferred_element_type=F32)
        yield
        m = jnp.max(logits, axis=0, keepdims=True)
        p = jnp.exp(logits - m)
        p = p / jnp.sum(p, axis=0, keepdims=True)
        nblk = aff_ref.shape[1] // N_EXPERTS
        blk = pl.program_id(1) * (tm // LANES) + r0 // LANES
        for e in range(N_EXPERTS):
            aff_ref[0, pl.ds(e * nblk + blk, 1), :] = p[e:e + 1, :]

    chains = [half_tile(r0) for r0 in range(0, tm, half)]
    while chains:
        chains = [ch for ch in chains if next(ch, "done") != "done"]


def _outproj(x, yf, yb, ua, dexp, snw, hnw, wout, nw1, nw2, g1, sh2, sc2, wrt):
    b, l, d = x.shape
    tm = TM
    assert tm % LANES == 0 and l % tm == 0
    nt = l // tm
    vec = lambda n: pl.BlockSpec((1, n), lambda i, j: (0, 0))
    mod = pl.BlockSpec((1, 1, d), lambda i, j: (i, 0, 0))
    usec = lambda sec: pl.BlockSpec((1, tm, 512), lambda i, j: (i, j, sec // 512))
    return pl.pallas_call(
        _outproj_kernel,
        out_shape=(jax.ShapeDtypeStruct((b, l, d), F32),
                   jax.ShapeDtypeStruct((b * l * SUBLANES, LANES), F32),
                   jax.ShapeDtypeStruct((b, N_EXPERTS * (l // LANES), LANES), F32)),
        grid=(b, nt),
        in_specs=[pl.BlockSpec((1, tm, d), lambda i, j: (i, j, 0)),
                  pl.BlockSpec((1, tm, 2 * SSD_WIDTH), lambda i, j: (i, j, 0)),
                  pl.BlockSpec((1, tm, 2 * SSD_WIDTH), lambda i, j: (i, j, 0)),
                  usec(A_XBC), usec(A_Z), usec(A_G),
                  vec(512), vec(512), vec(512),
                  pl.BlockSpec((d, d), lambda i, j: (0, 0)),
                  vec(d), vec(d), mod, mod, mod,
                  pl.BlockSpec((N_EXPERTS, d), lambda i, j: (0, 0))],
        out_specs=(pl.BlockSpec((1, tm, d), lambda i, j: (i, j, 0)),
                   pl.BlockSpec((tm * SUBLANES, LANES), lambda i, j: (i * nt + j, 0)),
                   pl.BlockSpec((1, N_EXPERTS * (l // LANES), LANES), lambda i, j: (i, 0, 0))),
        compiler_params=pltpu.CompilerParams(dimension_semantics=("arbitrary", "arbitrary"),
                                             vmem_limit_bytes=VMEM_LIMIT),
        name="outproj",
    )(x, yf, yb, ua, ua, ua, dexp, snw, hnw, wout, nw1, nw2, g1, sh2, sc2, wrt)


def _expert_kernel(idxc_ref, idxn_ref, h2_hbm, wg_ref, wu_ref, wd_ref, y_ref, xg, wgb, wub, wdb, sem,
                   *, n_tok, cap, nb, ne):
    i = pl.program_id(0)
    j = pl.program_id(1)
    step = i * nb + j
    slot = step % 2
    nxt = jnp.minimum(step + 1, ne * nb - 1)
    jn = lax.rem(nxt, nb)

    @pl.when(j == 0)
    def _():
        wgb[...] = wg_ref[0].astype(BF16)
        wub[...] = wu_ref[0].astype(BF16)
        wdb[...] = wd_ref[0].astype(BF16)

    n_chunk = cap // RC

    def row_copy(idx_ref, bb, sl, r, p):
        tok = bb * n_tok + idx_ref[0, 0, p]
        dst = pl.multiple_of((sl * cap + p) * SUBLANES, SUBLANES)
        return pltpu.make_async_copy(h2_hbm.at[tok], xg.at[pl.ds(dst, SUBLANES), :], sem.at[sl, r])

    def chunk_wait(sl, r):
        view = xg.at[pl.ds(pl.multiple_of((sl * cap + r * RC) * SUBLANES, SUBLANES), RC * SUBLANES), :]
        pltpu.make_async_copy(view, view, sem.at[sl, r]).wait()

    @pl.when(step == 0)
    def _():
        def issue(p, carry):
            row_copy(idxc_ref, j, slot, lax.div(p, RC), p).start()
            return carry
        lax.fori_loop(0, cap, issue, 0)

    fb = 256
    n_piece = wgb.shape[1] // fb
    per_piece = RC // (2 * n_piece)

    def chunk(r, carry):
        issued = [0]

        def issue_some():
            for q in range(issued[0], issued[0] + per_piece):
                row_copy(idxn_ref, jn, 1 - slot, r, r * RC + q).start()
            issued[0] += per_piece

        chunk_wait(slot, r)
        base = (slot * cap + r * RC) * SUBLANES
        xr = jnp.concatenate([xg[pl.ds(base + s, RC, stride=SUBLANES), :] for s in range(SUBLANES)],
                             axis=1).astype(BF16)
        acts = []
        for k in range(n_piece):
            ks = slice(k * fb, (k + 1) * fb)
            acts.append((_silu(_dot(xr, wgb[:, ks])) * _dot(xr, wub[:, ks])).astype(BF16))
            issue_some()
        y = None
        for k in range(n_piece):
            part = _dot(acts[k], wdb[k * fb:(k + 1) * fb, :])
            y = part if y is None else y + part
            issue_some()
        assert issued[0] == RC
        y_ref[0, 0, pl.ds(pl.multiple_of(r * RC, RC), RC), :] = y.astype(BF16)
        return carry

    lax.fori_loop(0, n_chunk, chunk, 0)

    @pl.when(step == ne * nb - 1)
    def _():
        for r in range(n_chunk):
            chunk_wait(1 - slot, r)


def _expert(idx, h2r, wg, wu, wd, nb, n_tok):
    ne, d, f = wg.shape
    cap = idx.shape[-1]

    def nxt_row(i, j):
        nxt = jnp.minimum(i * nb + j + 1, ne * nb - 1)
        return (lax.rem(nxt, nb) * ne + lax.div(nxt, nb), 0, 0)

    return pl.pallas_call(
        functools.partial(_expert_kernel, n_tok=n_tok, cap=cap, nb=nb, ne=ne),
        out_shape=jax.ShapeDtypeStruct((nb, ne, cap, d), BF16),
        grid=(ne, nb),
        in_specs=[pl.BlockSpec((1, 1, cap), lambda i, j: (j * ne + i, 0, 0), memory_space=pltpu.SMEM),
                  pl.BlockSpec((1, 1, cap), nxt_row, memory_space=pltpu.SMEM),
                  pl.BlockSpec(memory_space=pl.ANY),
                  pl.BlockSpec((1, d, f), lambda i, j: (i, 0, 0)),
                  pl.BlockSpec((1, d, f), lambda i, j: (i, 0, 0)),
                  pl.BlockSpec((1, f, d), lambda i, j: (i, 0, 0))],
        out_specs=pl.BlockSpec((1, 1, cap, d), lambda i, j: (j, i, 0, 0)),
        scratch_shapes=[pltpu.VMEM((2 * cap * SUBLANES, LANES), F32),
                        pltpu.VMEM((d, f), BF16), pltpu.VMEM((d, f), BF16), pltpu.VMEM((f, d), BF16),
                        pltpu.SemaphoreType.DMA((2, cap // RC))],
        compiler_params=pltpu.CompilerParams(dimension_semantics=("arbitrary", "arbitrary"),
                                             vmem_limit_bytes=VMEM_LIMIT),
        name="expert",
    )(idx, idx, h2r, wg, wu, wd)


def _combine_kernel(st_ref, x1_ref, pos_ref, aff_ref, y_hbm, nw_ref, g2_ref, o_ref, ywin, sem, *, nt, cap):
    i = pl.program_id(0)
    j = pl.program_id(1)
    step = i * nt + j
    total = pl.num_programs(0) * nt
    slot = lax.rem(step, N_SLOTS)
    ahead = N_SLOTS - 1
    tile_of = lambda s: (lax.div(s, nt), lax.rem(s, nt))

    def span(bi, jj, e):
        base = (bi * (nt + 1) + jj) * N_EXPERTS + e
        st = st_ref[base]
        st16 = lax.shift_left(lax.shift_right_logical(st, 4), 4)
        return st16, st_ref[base + N_EXPERTS] - st16

    def is_small(bi, jj):
        need = span(bi, jj, 0)[1]
        for e in range(1, N_EXPERTS):
            need = jnp.maximum(need, span(bi, jj, e)[1])
        return need <= WIN_S

    def win_start(bi, jj, e, w):
        return pl.multiple_of(jnp.minimum(span(bi, jj, e)[0], cap - w), BF16_ROWS)

    def win_copy(bi, jj, sl, e, w):
        return pltpu.make_async_copy(y_hbm.at[bi, e, pl.ds(win_start(bi, jj, e, w), w), :],
                                     ywin.at[sl, pl.ds(e * w, w), :], sem.at[sl])

    def issue(bi, jj, sl, small):
        for w, cond in ((WIN_S, small), (WIN, jnp.logical_not(small))):
            @pl.when(cond)
            def _():
                for e in range(N_EXPERTS):
                    win_copy(bi, jj, sl, e, w).start()

    small_c = is_small(i, j)

    @pl.when(step == 0)
    def _():
        for s in range(ahead):
            @pl.when(s < total)
            def _():
                bi, jj = tile_of(jnp.int32(s))
                issue(bi, jj, s, is_small(bi, jj))

    @pl.when(step + ahead < total)
    def _():
        bi, jj = tile_of(step + ahead)
        issue(bi, jj, lax.rem(step + ahead, N_SLOTS), is_small(bi, jj))

    pos = [pos_ref[0, pl.ds(e * nt + j, 1), :].astype(F32) for e in range(N_EXPERTS)]
    aff = [aff_ref[0, pl.ds(e * nt + j, 1), :] for e in range(N_EXPERTS)]

    def finish(w):
        for e in range(N_EXPERTS):
            win_copy(i, j, slot, e, w).wait()
        brow = lax.broadcasted_iota(jnp.int32, (LANES, 1), 0).astype(F32)
        target = [pos[e] + (e * w - win_start(i, j, e, w)).astype(F32) for e in range(N_EXPERTS)]
        blocks = []
        for c in range(N_EXPERTS * w // LANES):
            sel_t = jnp.zeros((LANES, TC), F32)
            for e in range(N_EXPERTS):
                if e * w < (c + 1) * LANES and (e + 1) * w > c * LANES:
                    sel_t = jnp.where(target[e] - float(c * LANES) == brow, aff[e], sel_t)
            blocks.append(sel_t.T.astype(BF16))
        acc = None
        kb = 2
        for c in range(0, len(blocks), kb):
            part = _dot(jnp.concatenate(blocks[c:c + kb], axis=1), ywin[slot, c * LANES:(c + kb) * LANES, :])
            acc = part if acc is None else acc + part
        o_ref[0] = x1_ref[0] + g2_ref[0] * _rms(acc, nw_ref[...])

    @pl.when(small_c)
    def _():
        finish(WIN_S)

    @pl.when(jnp.logical_not(small_c))
    def _():
        finish(WIN)


def _combine(starts, x1, pos_st, aff_st, y, nw, g2):
    b, l, d = x1.shape
    assert TC == LANES
    nt = l // TC
    cap = y.shape[2]
    stacked = pl.BlockSpec((1, N_EXPERTS * nt, LANES), lambda i, j, st: (i, 0, 0))
    grid_spec = pltpu.PrefetchScalarGridSpec(
        num_scalar_prefetch=1,
        grid=(b, nt),
        in_specs=[pl.BlockSpec((1, TC, d), lambda i, j, st: (i, j, 0)),
                  stacked, stacked,
                  pl.BlockSpec(memory_space=pl.ANY),
                  pl.BlockSpec((1, d), lambda i, j, st: (0, 0)),
                  pl.BlockSpec((1, 1, d), lambda i, j, st: (i, 0, 0))],
        out_specs=pl.BlockSpec((1, TC, d), lambda i, j, st: (i, j, 0)),
        scratch_shapes=[pltpu.VMEM((N_SLOTS, N_EXPERTS * WIN, d), BF16), pltpu.SemaphoreType.DMA((N_SLOTS,))],
    )
    return pl.pallas_call(
        functools.partial(_combine_kernel, nt=nt, cap=cap),
        out_shape=jax.ShapeDtypeStruct((b, l, d), F32),
        grid_spec=grid_spec,
        compiler_params=pltpu.CompilerParams(dimension_semantics=("arbitrary", "arbitrary"),
                                             vmem_limit_bytes=VMEM_LIMIT),
        name="combine",
    )(starts, x1, pos_st, aff_st, y, nw, g2)


def _route_kernel(aff_ref, pstrict_ref, pos_ref, idx_ref, st_ref, *, cap, nblk):
    rows = N_EXPERTS * nblk
    aff = aff_ref[0]

    def expert_total(x):
        t = jnp.sum(x.reshape(N_EXPERTS, nblk, LANES), axis=(1, 2), keepdims=True)
        return jnp.broadcast_to(t, (N_EXPERTS, nblk, LANES)).reshape(rows, LANES)

    def at_least_cap(cand):
        return expert_total((aff >= cand).astype(F32)) >= cap

    def search_bits(i, thr):
        cand = thr | lax.shift_left(jnp.int32(1), 30 - i)
        return jnp.where(at_least_cap(lax.bitcast_convert_type(cand, F32)), cand, thr)

    thr = lax.fori_loop(0, 31, search_bits, jnp.zeros((rows, LANES), jnp.int32))

    def search_mid(i, lo_hi):
        lo, hi = lo_hi
        mid = 0.5 * (lo + hi)
        ok = at_least_cap(mid)
        return jnp.where(ok, mid, lo), jnp.where(ok, hi, mid)

    lo, hi = lax.fori_loop(0, 24, search_mid,
                           (lax.bitcast_convert_type(thr, F32), lax.bitcast_convert_type(thr + 1, F32)))
    gt = aff >= hi
    eq = jnp.logical_and(aff >= lo, jnp.logical_not(gt))
    need = cap - expert_total(gt.astype(F32))

    r_i = lax.broadcasted_iota(jnp.int32, (LANES, LANES), 0)
    c_i = lax.broadcasted_iota(jnp.int32, (LANES, LANES), 1)
    upper = (r_i <= c_i).astype(BF16)
    ones = jnp.ones((LANES, LANES), BF16)

    def prefix(mask_bf):
        local = _dot(mask_bf, upper)
        tot = _dot(mask_bf, ones)
        return local, tot, _dot(pstrict_ref[...], tot.astype(BF16))

    eq_bf = eq.astype(BF16)
    l_eq, _, o_eq = prefix(eq_bf)
    sel = jnp.logical_or(gt, jnp.logical_and(eq, l_eq + o_eq - eq_bf.astype(F32) < need))
    l_sel, t_sel, o_sel = prefix(sel.astype(BF16))
    pos_ref[0] = jnp.where(sel, (l_sel + o_sel).astype(jnp.int32) - 1, -(1 << 20))
    st_ref[0] = o_sel.astype(jnp.int32)

    p_row = lax.broadcasted_iota(jnp.int32, (1, cap), 1).astype(F32)
    j_col = lax.broadcasted_iota(jnp.int32, (nblk, 1), 0).astype(F32)
    for e in range(N_EXPERTS):
        rs = slice(e * nblk, (e + 1) * nblk)
        off_c = o_sel[rs, 0:1]
        tot_c = t_sel[rs, 0:1]
        inblk = jnp.logical_and(off_c <= p_row, p_row < off_c + tot_c)
        q_row = p_row - jnp.sum(jnp.where(inblk, off_c, 0.0), axis=0, keepdims=True)
        j_row = jnp.sum(jnp.where(inblk, j_col, 0.0), axis=0, keepdims=True)
        l_t = jnp.concatenate([l_sel[rs, :], jnp.zeros((LANES - nblk, LANES), F32)], axis=0).T.astype(BF16)
        in_pad = jnp.concatenate([inblk.astype(BF16), jnp.zeros((LANES - nblk, cap), BF16)], axis=0)
        counts = _dot(l_t, in_pad)
        r_row = jnp.sum((counts <= q_row).astype(F32), axis=0, keepdims=True)
        idx_row = (j_row * LANES + r_row).astype(jnp.int32)
        for a in range(cap // LANES):
            idx_ref[0, e, a:a + 1, :] = idx_row[:, a * LANES:(a + 1) * LANES]


def _route(aff_st, cap):
    b, rows, _ = aff_st.shape
    nblk = rows // N_EXPERTS
    assert nblk <= LANES and cap % LANES == 0
    rho = jnp.arange(rows)
    pstrict = jnp.logical_and(rho[:, None] // nblk == rho[None, :] // nblk, rho[None, :] < rho[:, None]).astype(BF16)
    stacked = pl.BlockSpec((1, rows, LANES), lambda i: (i, 0, 0))
    pos_st, idx, st = pl.pallas_call(
        functools.partial(_route_kernel, cap=cap, nblk=nblk),
        out_shape=(jax.ShapeDtypeStruct((b, rows, LANES), jnp.int32),
                   jax.ShapeDtypeStruct((b, N_EXPERTS, cap // LANES, LANES), jnp.int32),
                   jax.ShapeDtypeStruct((b, rows, LANES), jnp.int32)),
        grid=(b,),
        in_specs=[stacked, pl.BlockSpec((rows, rows), lambda i: (0, 0))],
        out_specs=(stacked, pl.BlockSpec((1, N_EXPERTS, cap // LANES, LANES), lambda i: (i, 0, 0, 0)), stacked),
        compiler_params=pltpu.CompilerParams(dimension_semantics=("arbitrary",), vmem_limit_bytes=VMEM_LIMIT),
        name="route",
    )(aff_st, pstrict)
    starts = jnp.transpose(st[:, :, 0].reshape(b, N_EXPERTS, nblk), (0, 2, 1))
    starts = jnp.concatenate([starts, jnp.full((b, 1, N_EXPERTS), cap, jnp.int32)], axis=1)
    return idx.reshape(b * N_EXPERTS, 1, cap), pos_st, starts.reshape(-1)


def kernel(x, c, ctx, c_ctx, ada_w, ada_b, norm_w, w_in, ssd_conv_w, ssd_conv_b, ssd_dt_bias, ssd_a_log, ssd_d,
           ssd_norm_w, hgrn_lb, hgrn_norm_w, w_out, w_router, w_gate, w_up, w_down):
    b, l, d = x.shape
    lc = ctx.shape[1]
    layer = 0
    assert ada_w.shape[0] == 1, "single-layer stack"
    assert l % TM == 0 and lc % Q == 0

    cvec = jnp.zeros((SUBLANES, d), F32).at[:b].set(c).at[b].set(c_ctx)
    mod = _ada(cvec, ada_w[layer], ada_b[layer][None, :])
    ml = mod[:b].reshape(b, 1, 6, d)
    mc = jnp.broadcast_to(mod[b].reshape(1, 1, 6, d), (b, 1, 6, d))
    nw = norm_w[layer]

    wi = w_in[layer]
    o_xbc, o_dt, o_q = SSD_WIDTH, SSD_WIDTH + XBC, SSD_WIDTH + XBC + 2 * SSD_HEADS
    sec = lambda k: wi[:, o_q + k * 512:o_q + (k + 1) * 512]
    w_r = jnp.concatenate([wi[:, o_xbc:o_dt], sec(0), sec(3), wi[:, :SSD_WIDTH], sec(4), sec(1), sec(2),
                           jnp.pad(wi[:, o_dt:o_q], ((0, 0), (0, LANES - 2 * SSD_HEADS)))], axis=1).astype(BF16)
    pad16 = lambda v: jnp.pad(v.reshape(1, 2 * SSD_HEADS), ((0, 0), (0, LANES - 2 * SSD_HEADS)))
    dtb = pad16(ssd_dt_bias[layer])
    alog = pad16(ssd_a_log[layer])
    lane_id = jnp.arange(LANES)[None, :, None]
    chan_head = (jnp.arange(SSD_WIDTH) // SSD_HEADDIM)[None, None, :]
    expm = (lane_id == chan_head + SSD_HEADS * jnp.arange(2)[:, None, None]).astype(BF16)
    lbs = jnp.cumsum(jax.nn.softmax(hgrn_lb.astype(F32), axis=0), axis=0)[layer]
    cw, cb = ssd_conv_w[layer], ssd_conv_b[layer][None, :]

    ca, cbb = _inproj(ctx, mc[:, :, 0], mc[:, :, 1], nw[0:1], w_r, cw, cb, dtb, lbs[0:1], lbs[1:2], lc)
    ua, ub = _inproj(x, ml[:, :, 0], ml[:, :, 1], nw[0:1], w_r, cw, cb, dtb, lbs[0:1], lbs[1:2], GRID_W)

    s0s = jnp.zeros((b, 2, SSD_GROUPS, GW, SSD_STATE), F32)
    s0h = jnp.zeros((b, 2, HGRN_HEADS, HGRN_HEADDIM, HGRN_HEADDIM), F32)
    _, _, scs, sch = _mix(ca, cbb, alog, expm, s0s, s0h)
    yf, yb, _, _ = _mix(ua, ub, alog, expm, scs, sch)

    dexp = jnp.repeat(ssd_d[layer].astype(F32), SSD_HEADDIM)[None, :]
    x1, h2r, aff = _outproj(x, yf, yb, ua, dexp, ssd_norm_w[layer][None, :], hgrn_norm_w[layer][None, :],
                            w_out[layer].astype(BF16), nw[1:2], nw[2:3], ml[:, :, 2], ml[:, :, 3], ml[:, :, 4],
                            w_router[layer].T)

    cap = CAPACITY_FACTOR * l // N_EXPERTS
    assert cap % RC == 0 and cap >= WIN
    idx, pos_st, starts = _route(aff, cap)
    y = _expert(idx, h2r.reshape(b * l, SUBLANES, LANES), w_gate[layer], w_up[layer], w_down[layer], b, l)
    return _combine(starts, x1, pos_st, aff, y, nw[3:4], ml[:, :, 5])
```

```python
import functools

import jax
import jax.numpy as jnp
from jax import lax
from jax.experimental import pallas as pl
from jax.experimental.pallas import tpu as pltpu

F32 = jnp.float32
BF16 = jnp.bfloat16
EPS = 1e-6

LANES = 128
SUBLANES = 8
BF16_ROWS = 16
VMEM_LIMIT = 56 * 1024 * 1024

GRID_W = 64
SSD_WIDTH, SSD_HEADS, SSD_HEADDIM, SSD_GROUPS, SSD_STATE = 512, 8, 64, 2, 128
HGRN_WIDTH, HGRN_HEADS, HGRN_HEADDIM, HGRN_CHUNK = 512, 4, 128, 64
XBC = SSD_WIDTH + 2 * SSD_GROUPS * SSD_STATE
HG = SSD_HEADS // SSD_GROUPS
GW = HG * SSD_HEADDIM
N_EXPERTS = 16
CAPACITY_FACTOR = 2

A_XBC, A_Q, A_V, A_Z, A_G, A_COLS = 0, 1024, 1536, 2048, 2560, 3072
B_FF, B_FB, B_DT, B_COLS = 0, 512, 1024, 1024 + LANES

Q = 128
TM = 512
TM_IN = 256
TC = 128
WIN = TC + BF16_ROWS
WIN_S = 32 + BF16_ROWS
RC = 512
N_SLOTS = 3
MIX_SAMPLES = 4


def _dot(a, b):
    return jnp.dot(a, b, preferred_element_type=F32)


def _dot_nt(a, b):
    return lax.dot_general(a, b, (((1,), (1,)), ((), ())), preferred_element_type=F32)


def _dot_tn(a, b):
    return _dot(a.T.astype(BF16), b)


def _split2(x):
    x1 = x.astype(BF16)
    return x1, (x - x1.astype(F32)).astype(BF16)


def _dot01(m01, x):
    x1, x2 = _split2(x)
    return _dot(m01, x1) + _dot(m01, x2)


def _silu(x):
    return x * jax.nn.sigmoid(x)


def _rms(x, w):
    return x * lax.rsqrt(jnp.mean(x * x, axis=-1, keepdims=True) + EPS) * w


def _ada_kernel(c_ref, w_ref, b_ref, o_ref):
    c = c_ref[...]
    o_ref[...] = jnp.dot(_silu(c), w_ref[...], precision=lax.Precision.HIGHEST,
                         preferred_element_type=F32) + b_ref[...]


def _ada(cvec, w, b):
    rows, d = cvec.shape
    n = w.shape[1]
    tn = 1024
    return pl.pallas_call(
        _ada_kernel,
        out_shape=jax.ShapeDtypeStruct((rows, n), F32),
        grid=(n // tn,),
        in_specs=[pl.BlockSpec((rows, d), lambda j: (0, 0)),
                  pl.BlockSpec((d, tn), lambda j: (0, j)),
                  pl.BlockSpec((1, tn), lambda j: (0, j))],
        out_specs=pl.BlockSpec((rows, tn), lambda j: (0, j)),
        compiler_params=pltpu.CompilerParams(dimension_semantics=("arbitrary",), vmem_limit_bytes=VMEM_LIMIT),
        name="ada",
    )(cvec, w, b)


def _inproj_kernel(x_ref, sh_ref, sc_ref, nw_ref, w_ref, cw_ref, cb_ref, dtb_ref, lbf_ref, lbb_ref,
                   a_ref, b_ref, *, row_len):
    x = x_ref[0]
    h = _rms(x, nw_ref[...]) * (1.0 + sc_ref[0]) + sh_ref[0]
    hb = h.astype(BF16)
    tm = x.shape[0]

    pw = 256

    def conv_silu(c0):
        def finish(xbc):
            pos = lax.broadcasted_iota(jnp.int32, (tm, 1), 0) % row_len
            cw = cw_ref[:, c0:c0 + pw]
            acc = cb_ref[:, c0:c0 + pw] + cw[2:3, :] * xbc
            for d in (-2, -1, 1, 2):
                shifted = pltpu.roll(xbc, (-d) % tm, 0)
                ok = jnp.logical_and(pos + d >= 0, pos + d < row_len)
                acc = acc + cw[2 + d:3 + d, :] * jnp.where(ok, shifted, 0.0)
            a_ref[0, :, A_XBC + c0:A_XBC + c0 + pw] = _silu(acc).astype(BF16)
        return finish

    def put_a(col, fn):
        def finish(r):
            a_ref[0, :, col:col + pw] = fn(r).astype(BF16)
        return finish

    def put_gate(col, lb_ref, c0):
        def finish(r):
            lb = lb_ref[:, c0:c0 + pw]
            b_ref[0, :, col:col + pw] = lb + (1.0 - lb) * jax.nn.sigmoid(r)
        return finish

    def put_dt(r):
        dtr = r + dtb_ref[...]
        b_ref[0, :, B_DT:B_DT + LANES] = jnp.maximum(dtr, 0.0) + jnp.log(1.0 + jnp.exp(-jnp.abs(dtr)))

    wb = A_COLS
    pieces = [(A_XBC + c0, pw, conv_silu(c0)) for c0 in range(0, XBC, pw)]
    for sec, fn in ((A_Q, _silu), (A_V, lambda r: r), (A_Z, _silu), (A_G, _silu)):
        pieces += [(sec + c0, pw, put_a(sec + c0, fn)) for c0 in range(0, 512, pw)]
    for sec, lb_ref in ((B_FF, lbf_ref), (B_FB, lbb_ref)):
        pieces += [(wb + sec + c0, pw, put_gate(sec + c0, lb_ref, c0)) for c0 in range(0, 512, pw)]
    pieces.append((wb + B_DT, LANES, put_dt))
    pending = None
    for col, width, finish in pieces:
        r = _dot(hb, w_ref[:, col:col + width])
        if pending is not None:
            pending[0](pending[1])
        pending = (finish, r)
    pending[0](pending[1])


def _inproj(x, shift, scale, nw, w_r, cw, cb, dtb, lbf, lbb, row_len):
    b, l, d = x.shape
    tm = min(TM_IN, l)
    assert l % tm == 0 and tm % row_len == 0
    vec = lambda n: pl.BlockSpec((1, n), lambda i, j: (0, 0))
    return pl.pallas_call(
        functools.partial(_inproj_kernel, row_len=row_len),
        out_shape=(jax.ShapeDtypeStruct((b, l, A_COLS), BF16), jax.ShapeDtypeStruct((b, l, B_COLS), F32)),
        grid=(b, l // tm),
        in_specs=[pl.BlockSpec((1, tm, d), lambda i, j: (i, j, 0)),
                  pl.BlockSpec((1, 1, d), lambda i, j: (i, 0, 0)),
                  pl.BlockSpec((1, 1, d), lambda i, j: (i, 0, 0)),
                  vec(d),
                  pl.BlockSpec((d, A_COLS + B_COLS), lambda i, j: (0, 0)),
                  pl.BlockSpec((5, XBC), lambda i, j: (0, 0)),
                  vec(XBC), vec(LANES), vec(512), vec(512)],
        out_specs=(pl.BlockSpec((1, tm, A_COLS), lambda i, j: (i, j, 0)),
                   pl.BlockSpec((1, tm, B_COLS), lambda i, j: (i, j, 0))),
        compiler_params=pltpu.CompilerParams(dimension_semantics=("arbitrary", "arbitrary"),
                                             vmem_limit_bytes=VMEM_LIMIT),
        name="inproj",
    )(x, shift, scale, nw, w_r, cw, cb, dtb, lbf, lbb)


def _masks(d):
    row = lax.broadcasted_iota(jnp.int32, (Q, Q), 0)
    col = lax.broadcasted_iota(jnp.int32, (Q, Q), 1)
    tri = (col <= row) if d == 0 else (col >= row)
    blk = jnp.logical_and(tri, (row // HGRN_CHUNK) == (col // HGRN_CHUNK))
    return tri, blk


def _ssd_chain(d, n, xbc_ref, dt_ref, alog_ref, expm_ref, y_ref, ss_ref):
    tri, _ = _masks(d)
    tri_bf = tri.astype(BF16)
    xbc = xbc_ref[n]
    bm = xbc[:, SSD_WIDTH:SSD_WIDTH + SSD_GROUPS * SSD_STATE]
    cm = xbc[:, SSD_WIDTH + SSD_GROUPS * SSD_STATE:]
    dt = dt_ref[n]
    spread = lambda v: _dot(v.astype(BF16), expm_ref[d])
    dt_x = spread(dt)
    a_sm = _dot01(tri_bf, dt * -jnp.exp(alog_ref[...]))
    bgs = [bm[:, g * SSD_STATE:(g + 1) * SSD_STATE] for g in range(SSD_GROUPS)]
    cgs = [cm[:, g * SSD_STATE:(g + 1) * SSD_STATE] for g in range(SSD_GROUPS)]
    s_ins = [ss_ref[n, d, g] for g in range(SSD_GROUPS)]
    emit_y = y_ref is not None
    if emit_y:
        cbs = [_dot_nt(cgs[g], bgs[g]) for g in range(SSD_GROUPS)]
        y_offs = [_dot_nt(cgs[g], s_ins[g].astype(BF16)) for g in range(SSD_GROUPS)]
    yield
    tot = Q - 1 if d == 0 else 0
    d_end = spread(jnp.exp(a_sm[tot:tot + 1, :] - a_sm))
    xdt = xbc[:, :SSD_WIDTH].astype(F32) * dt_x
    if emit_y:
        a_sm_t = a_sm.T
        e_cs = spread(jnp.exp(a_sm))
        lane_head = lax.broadcasted_iota(jnp.int32, (Q, GW), 1) // SSD_HEADDIM
        y_ds = []
        for g in range(SSD_GROUPS):
            xdt_g = xdt[:, g * GW:(g + 1) * GW]
            w_cat, x_cat = [], []
            for j in range(HG):
                hl = d * SSD_HEADS + g * HG + j
                lmat = jnp.exp(jnp.where(tri, a_sm[:, hl:hl + 1] - a_sm_t[hl:hl + 1, :], -jnp.inf))
                w_cat.append((cbs[g] * lmat).astype(BF16))
                x_cat.append(jnp.where(lane_head == j, xdt_g, 0.0).astype(BF16))
            y_ds.append(_dot(jnp.concatenate(w_cat, axis=1), jnp.concatenate(x_cat, axis=0)))
            yield
    xdec = xdt * d_end
    sts = [_dot_tn(xdec[:, g * GW:(g + 1) * GW], bgs[g]) for g in range(SSD_GROUPS)]
    if emit_y:
        for g in range(SSD_GROUPS):
            gs = slice(g * GW, (g + 1) * GW)
            y_ref[n, :, gs] = (y_ds[g] + y_offs[g] * e_cs[:, gs]).astype(y_ref.dtype)
    yield
    for g in range(SSD_GROUPS):
        dtots = [jnp.broadcast_to(jnp.exp(a_sm[tot:tot + 1, hl:hl + 1]), (SSD_HEADDIM, SSD_STATE))
                 for hl in range(d * SSD_HEADS + g * HG, d * SSD_HEADS + (g + 1) * HG)]
        ss_ref[n, d, g] = jnp.concatenate(dtots, axis=0) * s_ins[g] + sts[g]


def _hgrn_chain(d, n, q_ref, v_ref, f_ref, y_ref, sh_ref):
    _, blk = _masks(d)
    ff = f_ref[n]
    bcs = _dot01(blk.astype(BF16), jnp.log(ff))
    vv_bf = v_ref[n]
    heads = [slice(h * HGRN_HEADDIM, (h + 1) * HGRN_HEADDIM) for h in range(HGRN_HEADS)]
    v_ts = [vv_bf[:, sl].astype(F32).T.astype(BF16) for sl in heads]
    yield
    q_dec_bf = (q_ref[n].astype(F32) * jnp.exp(bcs)).astype(BF16)
    kk = 1.0 - ff
    k_inv = (kk * jnp.exp(-bcs)).astype(BF16)
    if d == 0:
        tots = (bcs[HGRN_CHUNK - 1:HGRN_CHUNK, :], bcs[Q - 1:Q, :])
    else:
        tots = (bcs[0:1, :], bcs[HGRN_CHUNK:HGRN_CHUNK + 1, :])
    tot_full = jnp.concatenate([jnp.broadcast_to(tots[0], (HGRN_CHUNK, HGRN_WIDTH)),
                                jnp.broadcast_to(tots[1], (HGRN_CHUNK, HGRN_WIDTH))], axis=0)
    k_end = kk * jnp.exp(tot_full - bcs)
    in_first = lax.broadcasted_iota(jnp.int32, (Q, 1), 0) < HGRN_CHUNK
    k_ends = (jnp.where(in_first, k_end, 0.0).astype(BF16), jnp.where(in_first, 0.0, k_end).astype(BF16))
    c0, c1 = (0, 1) if d == 0 else (1, 0)
    rows = lambda ci: slice(ci * HGRN_CHUNK, (ci + 1) * HGRN_CHUNK)
    s_ts = [sh_ref[n, d, h] for h in range(HGRN_HEADS)]
    emit_y = y_ref is not None
    if emit_y:
        atts = [_dot_nt(q_dec_bf[:, sl], k_inv[:, sl]) for sl in heads]
        o_first = [_dot_nt(q_dec_bf[rows(c0), sl], s_ts[h].astype(BF16)) for h, sl in enumerate(heads)]
    upd = [_dot(v_ts[h], k_ends[c0][:, sl]) for h, sl in enumerate(heads)]
    yield
    if emit_y:
        o_intra = [_dot(jnp.where(blk, atts[h], 0.0).astype(BF16), vv_bf[:, sl]) for h, sl in enumerate(heads)]
    s_ts = [jnp.exp(tots[c0][:, sl]) * s_ts[h] + upd[h] for h, sl in enumerate(heads)]
    if emit_y:
        o_second = [_dot_nt(q_dec_bf[rows(c1), sl], s_ts[h].astype(BF16)) for h, sl in enumerate(heads)]
    upd = [_dot(v_ts[h], k_ends[c1][:, sl]) for h, sl in enumerate(heads)]
    yield
    for h, sl in enumerate(heads):
        sh_ref[n, d, h] = jnp.exp(tots[c1][:, sl]) * s_ts[h] + upd[h]
        if not emit_y:
            continue
        outs = [None, None]
        outs[c0] = o_intra[h][rows(c0)] + o_first[h]
        outs[c1] = o_intra[h][rows(c1)] + o_second[h]
        y_ref[n, :, SSD_WIDTH + h * HGRN_HEADDIM:SSD_WIDTH + (h + 1) * HGRN_HEADDIM] = (
            jnp.concatenate(outs, axis=0).astype(y_ref.dtype))


def _mix_kernel(xbc_f, q_f, v_f, f_f, dt_f, xbc_b, q_b, v_b, f_b, dt_b, alog_ref, expm_ref,
                s0s_ref, s0h_ref, *out_refs):
    yf_ref, yb_ref = out_refs[:-2] if len(out_refs) == 4 else (None, None)
    ss_ref, sh_ref = out_refs[-2:]

    @pl.when(pl.program_id(1) == 0)
    def _():
        ss_ref[...] = s0s_ref[...]
        sh_ref[...] = s0h_ref[...]

    chains = []
    for n in range(ss_ref.shape[0]):
        chains.append(_ssd_chain(0, n, xbc_f, dt_f, alog_ref, expm_ref, yf_ref, ss_ref))
        chains.append(_hgrn_chain(0, n, q_f, v_f, f_f, yf_ref, sh_ref))
        chains.append(_ssd_chain(1, n, xbc_b, dt_b, alog_ref, expm_ref, yb_ref, ss_ref))
        chains.append(_hgrn_chain(1, n, q_b, v_b, f_b, yb_ref, sh_ref))
    while chains:
        alive = []
        for ch in chains:
            if next(ch, "done") != "done":
                alive.append(ch)
        chains = alive


def _mix(ua, ub, alog, expm, s0s, s0h, emit_y):
    b, l, _ = ua.shape
    nc = l // Q
    ns = MIX_SAMPLES if b % MIX_SAMPLES == 0 else 1
    fwd = lambda w, sec: pl.BlockSpec((ns, Q, w), lambda i, c: (i, c, sec // w))
    bwd = lambda w, sec: pl.BlockSpec((ns, Q, w), lambda i, c: (i, nc - 1 - c, sec // w))
    ss_spec = pl.BlockSpec((ns,) + s0s.shape[1:], lambda i, c: (i, 0, 0, 0, 0))
    sh_spec = pl.BlockSpec((ns,) + s0h.shape[1:], lambda i, c: (i, 0, 0, 0, 0))
    y_shapes = (jax.ShapeDtypeStruct((b, l, 2 * SSD_WIDTH), BF16),) * 2 if emit_y else ()
    y_specs = ((pl.BlockSpec((ns, Q, 2 * SSD_WIDTH), lambda i, c: (i, c, 0)),
                pl.BlockSpec((ns, Q, 2 * SSD_WIDTH), lambda i, c: (i, nc - 1 - c, 0))) if emit_y else ())
    return pl.pallas_call(
        _mix_kernel,
        out_shape=y_shapes + (jax.ShapeDtypeStruct(s0s.shape, F32), jax.ShapeDtypeStruct(s0h.shape, F32)),
        grid=(b // ns, nc),
        in_specs=[fwd(XBC, A_XBC), fwd(512, A_Q), fwd(512, A_V), fwd(512, B_FF), fwd(LANES, B_DT),
                  bwd(XBC, A_XBC), bwd(512, A_Q), bwd(512, A_V), bwd(512, B_FB), bwd(LANES, B_DT),
                  pl.BlockSpec((1, LANES), lambda i, c: (0, 0)),
                  pl.BlockSpec((2, LANES, SSD_WIDTH), lambda i, c: (0, 0, 0)),
                  ss_spec, sh_spec],
        out_specs=y_specs + (ss_spec, sh_spec),
        compiler_params=pltpu.CompilerParams(dimension_semantics=("arbitrary", "arbitrary"),
                                             vmem_limit_bytes=VMEM_LIMIT),
        name="mix",
    )(ua, ua, ua, ub, ub, ua, ua, ua, ub, ub, alog, expm, s0s, s0h)


def _outproj_kernel(x_ref, yf_ref, yb_ref, xs_ref, z_ref, g_ref, dexp_ref, snw_ref, hnw_ref, wout_ref,
                    nw1_ref, nw2_ref, g1_ref, sh2_ref, sc2_ref, wrt_ref,
                    x1_ref, h2_ref, aff_ref):
    tm = x_ref.shape[1]
    half = LANES
    snw = snw_ref[...]
    hnw = hnw_ref[...]

    def half_tile(r0):
        rs = pl.ds(r0, half)
        y = yf_ref[0, rs, :].astype(F32) + yb_ref[0, rs, :].astype(F32)
        ys = (y[:, :SSD_WIDTH] + dexp_ref[...] * xs_ref[0, rs, :].astype(F32)) * z_ref[0, rs, :].astype(F32)
        parts = []
        for g in range(SSD_GROUPS):
            parts.append(_rms(ys[:, g * GW:(g + 1) * GW], snw[:, g * GW:(g + 1) * GW]))
        yh = y[:, SSD_WIDTH:]
        gate = g_ref[0, rs, :].astype(F32)
        for h in range(HGRN_HEADS):
            sl = slice(h * HGRN_HEADDIM, (h + 1) * HGRN_HEADDIM)
            parts.append(_rms(yh[:, sl], hnw[:, sl]) * gate[:, sl])
        ycat = jnp.concatenate(parts, axis=1).astype(BF16)
        proj = _dot(ycat, wout_ref[...])
        yield
        x1 = x_ref[0, rs, :] + g1_ref[0] * _rms(proj, nw1_ref[...])
        x1_ref[0, rs, :] = x1
        h2 = _rms(x1, nw2_ref[...]) * (1.0 + sc2_ref[0]) + sh2_ref[0]
        for s in range(SUBLANES):
            h2_ref[pl.ds(r0 * SUBLANES + s, half, stride=SUBLANES), :] = h2[:, s * LANES:(s + 1) * LANES]
        logits = lax.dot_general(wrt_ref[...], h2, (((1,), (1,)), ((), ())), precision=lax.Precision.HIGHEST,
                                 preferred_element_type=F32)
        yield
        m = jnp.max(logits, axis=0, keepdims=True)
        p = jnp.exp(logits - m)
        p = p / jnp.sum(p, axis=0, keepdims=True)
        nblk = aff_ref.shape[1] // N_EXPERTS
        blk = pl.program_id(1) * (tm // LANES) + r0 // LANES
        for e in range(N_EXPERTS):
            aff_ref[0, pl.ds(e * nblk + blk, 1), :] = p[e:e + 1, :]

    chains = [half_tile(r0) for r0 in range(0, tm, half)]
    while chains:
        chains = [ch for ch in chains if next(ch, "done") != "done"]


def _outproj(x, yf, yb, ua, dexp, snw, hnw, wout, nw1, nw2, g1, sh2, sc2, wrt):
    b, l, d = x.shape
    tm = TM
    assert tm % LANES == 0 and l % tm == 0
    nt = l // tm
    vec = lambda n: pl.BlockSpec((1, n), lambda i, j: (0, 0))
    mod = pl.BlockSpec((1, 1, d), lambda i, j: (i, 0, 0))
    usec = lambda sec: pl.BlockSpec((1, tm, 512), lambda i, j: (i, j, sec // 512))
    return pl.pallas_call(
        _outproj_kernel,
        out_shape=(jax.ShapeDtypeStruct((b, l, d), F32),
                   jax.ShapeDtypeStruct((b * l * SUBLANES, LANES), F32),
                   jax.ShapeDtypeStruct((b, N_EXPERTS * (l // LANES), LANES), F32)),
        grid=(b, nt),
        in_specs=[pl.BlockSpec((1, tm, d), lambda i, j: (i, j, 0)),
                  pl.BlockSpec((1, tm, 2 * SSD_WIDTH), lambda i, j: (i, j, 0)),
                  pl.BlockSpec((1, tm, 2 * SSD_WIDTH), lambda i, j: (i, j, 0)),
                  usec(A_XBC), usec(A_Z), usec(A_G),
                  vec(512), vec(512), vec(512),
                  pl.BlockSpec((d, d), lambda i, j: (0, 0)),
                  vec(d), vec(d), mod, mod, mod,
                  pl.BlockSpec((N_EXPERTS, d), lambda i, j: (0, 0))],
        out_specs=(pl.BlockSpec((1, tm, d), lambda i, j: (i, j, 0)),
                   pl.BlockSpec((tm * SUBLANES, LANES), lambda i, j: (i * nt + j, 0)),
                   pl.BlockSpec((1, N_EXPERTS * (l // LANES), LANES), lambda i, j: (i, 0, 0))),
        compiler_params=pltpu.CompilerParams(dimension_semantics=("arbitrary", "arbitrary"),
                                             vmem_limit_bytes=VMEM_LIMIT),
        name="outproj",
    )(x, yf, yb, ua, ua, ua, dexp, snw, hnw, wout, nw1, nw2, g1, sh2, sc2, wrt)


def _expert_kernel(idxc_ref, idxn_ref, h2_hbm, wg_ref, wu_ref, wd_ref, y_ref, xg, wgb, wub, wdb, sem,
                   *, n_tok, cap, nb, ne):
    i = pl.program_id(0)
    j = pl.program_id(1)
    step = i * nb + j
    slot = step % 2
    nxt = jnp.minimum(step + 1, ne * nb - 1)
    jn = lax.rem(nxt, nb)

    @pl.when(j == 0)
    def _():
        wgb[...] = wg_ref[0].astype(BF16)
        wub[...] = wu_ref[0].astype(BF16)
        wdb[...] = wd_ref[0].astype(BF16)

    n_chunk = cap // RC

    def row_copy(idx_ref, bb, sl, r, p):
        tok = bb * n_tok + idx_ref[0, 0, p]
        dst = pl.multiple_of((sl * cap + p) * SUBLANES, SUBLANES)
        return pltpu.make_async_copy(h2_hbm.at[tok], xg.at[pl.ds(dst, SUBLANES), :], sem.at[sl, r])

    def chunk_wait(sl, r):
        view = xg.at[pl.ds(pl.multiple_of((sl * cap + r * RC) * SUBLANES, SUBLANES), RC * SUBLANES), :]
        pltpu.make_async_copy(view, view, sem.at[sl, r]).wait()

    @pl.when(step == 0)
    def _():
        def issue(p, carry):
            row_copy(idxc_ref, j, slot, lax.div(p, RC), p).start()
            return carry
        lax.fori_loop(0, cap, issue, 0)

    fb = 256
    n_piece = wgb.shape[1] // fb
    per_piece = RC // (2 * n_piece)

    def chunk(r, carry):
        issued = [0]

        def issue_some():
            for q in range(issued[0], issued[0] + per_piece):
                row_copy(idxn_ref, jn, 1 - slot, r, r * RC + q).start()
            issued[0] += per_piece

        chunk_wait(slot, r)
        base = (slot * cap + r * RC) * SUBLANES
        xr = jnp.concatenate([xg[pl.ds(base + s, RC, stride=SUBLANES), :] for s in range(SUBLANES)],
                             axis=1).astype(BF16)
        acts = []
        for k in range(n_piece):
            ks = slice(k * fb, (k + 1) * fb)
            acts.append((_silu(_dot(xr, wgb[:, ks])) * _dot(xr, wub[:, ks])).astype(BF16))
            issue_some()
        y = None
        for k in range(n_piece):
            part = _dot(acts[k], wdb[k * fb:(k + 1) * fb, :])
            y = part if y is None else y + part
            issue_some()
        assert issued[0] == RC
        y_ref[0, 0, pl.ds(pl.multiple_of(r * RC, RC), RC), :] = y.astype(BF16)
        return carry

    lax.fori_loop(0, n_chunk, chunk, 0)

    @pl.when(step == ne * nb - 1)
    def _():
        for r in range(n_chunk):
            chunk_wait(1 - slot, r)


def _expert(idx, h2r, wg, wu, wd, nb, n_tok):
    ne, d, f = wg.shape
    cap = idx.shape[-1]

    def nxt_row(i, j):
        nxt = jnp.minimum(i * nb + j + 1, ne * nb - 1)
        return (lax.rem(nxt, nb) * ne + lax.div(nxt, nb), 0, 0)

    return pl.pallas_call(
        functools.partial(_expert_kernel, n_tok=n_tok, cap=cap, nb=nb, ne=ne),
        out_shape=jax.ShapeDtypeStruct((nb, ne, cap, d), BF16),
        grid=(ne, nb),
        in_specs=[pl.BlockSpec((1, 1, cap), lambda i, j: (j * ne + i, 0, 0), memory_space=pltpu.SMEM),
                  pl.BlockSpec((1, 1, cap), nxt_row, memory_space=pltpu.SMEM),
                  pl.BlockSpec(memory_space=pl.ANY),
                  pl.BlockSpec((1, d, f), lambda i, j: (i, 0, 0)),
                  pl.BlockSpec((1, d, f), lambda i, j: (i, 0, 0)),
                  pl.BlockSpec((1, f, d), lambda i, j: (i, 0, 0))],
        out_specs=pl.BlockSpec((1, 1, cap, d), lambda i, j: (j, i, 0, 0)),
        scratch_shapes=[pltpu.VMEM((2 * cap * SUBLANES, LANES), F32),
                        pltpu.VMEM((d, f), BF16), pltpu.VMEM((d, f), BF16), pltpu.VMEM((f, d), BF16),
                        pltpu.SemaphoreType.DMA((2, cap // RC))],
        compiler_params=pltpu.CompilerParams(dimension_semantics=("arbitrary", "arbitrary"),
                                             vmem_limit_bytes=VMEM_LIMIT),
        name="expert",
    )(idx, idx, h2r, wg, wu, wd)


def _combine_kernel(st_ref, x1_ref, pos_ref, aff_ref, y_hbm, nw_ref, g2_ref, o_ref, ywin, sem, *, nt, cap):
    i = pl.program_id(0)
    j = pl.program_id(1)
    step = i * nt + j
    total = pl.num_programs(0) * nt
    slot = lax.rem(step, N_SLOTS)
    ahead = N_SLOTS - 1
    tile_of = lambda s: (lax.div(s, nt), lax.rem(s, nt))

    def span(bi, jj, e):
        base = (bi * (nt + 1) + jj) * N_EXPERTS + e
        st = st_ref[base]
        st16 = lax.shift_left(lax.shift_right_logical(st, 4), 4)
        return st16, st_ref[base + N_EXPERTS] - st16

    def is_small(bi, jj):
        need = span(bi, jj, 0)[1]
        for e in range(1, N_EXPERTS):
            need = jnp.maximum(need, span(bi, jj, e)[1])
        return need <= WIN_S

    def win_start(bi, jj, e, w):
        return pl.multiple_of(jnp.minimum(span(bi, jj, e)[0], cap - w), BF16_ROWS)

    def win_copy(bi, jj, sl, e, w):
        return pltpu.make_async_copy(y_hbm.at[bi, e, pl.ds(win_start(bi, jj, e, w), w), :],
                                     ywin.at[sl, pl.ds(e * w, w), :], sem.at[sl])

    def issue(bi, jj, sl, small):
        for w, cond in ((WIN_S, small), (WIN, jnp.logical_not(small))):
            @pl.when(cond)
            def _():
                for e in range(N_EXPERTS):
                    win_copy(bi, jj, sl, e, w).start()

    small_c = is_small(i, j)

    @pl.when(step == 0)
    def _():
        for s in range(ahead):
            @pl.when(s < total)
            def _():
                bi, jj = tile_of(jnp.int32(s))
                issue(bi, jj, s, is_small(bi, jj))

    @pl.when(step + ahead < total)
    def _():
        bi, jj = tile_of(step + ahead)
        issue(bi, jj, lax.rem(step + ahead, N_SLOTS), is_small(bi, jj))

    pos = [pos_ref[0, pl.ds(e * nt + j, 1), :].astype(F32) for e in range(N_EXPERTS)]
    aff = [aff_ref[0, pl.ds(e * nt + j, 1), :] for e in range(N_EXPERTS)]

    def finish(w):
        for e in range(N_EXPERTS):
            win_copy(i, j, slot, e, w).wait()
        brow = lax.broadcasted_iota(jnp.int32, (LANES, 1), 0).astype(F32)
        target = [pos[e] + (e * w - win_start(i, j, e, w)).astype(F32) for e in range(N_EXPERTS)]
        blocks = []
        for c in range(N_EXPERTS * w // LANES):
            sel_t = jnp.zeros((LANES, TC), F32)
            for e in range(N_EXPERTS):
                if e * w < (c + 1) * LANES and (e + 1) * w > c * LANES:
                    sel_t = jnp.where(target[e] - float(c * LANES) == brow, aff[e], sel_t)
            blocks.append(sel_t.T.astype(BF16))
        acc = None
        kb = 2
        for c in range(0, len(blocks), kb):
            part = _dot(jnp.concatenate(blocks[c:c + kb], axis=1), ywin[slot, c * LANES:(c + kb) * LANES, :])
            acc = part if acc is None else acc + part
        o_ref[0] = x1_ref[0] + g2_ref[0] * _rms(acc, nw_ref[...])

    @pl.when(small_c)
    def _():
        finish(WIN_S)

    @pl.when(jnp.logical_not(small_c))
    def _():
        finish(WIN)


def _combine(starts, x1, pos_st, aff_st, y, nw, g2):
    b, l, d = x1.shape
    assert TC == LANES
    nt = l // TC
    cap = y.shape[2]
    stacked = pl.BlockSpec((1, N_EXPERTS * nt, LANES), lambda i, j, st: (i, 0, 0))
    grid_spec = pltpu.PrefetchScalarGridSpec(
        num_scalar_prefetch=1,
        grid=(b, nt),
        in_specs=[pl.BlockSpec((1, TC, d), lambda i, j, st: (i, j, 0)),
                  stacked, stacked,
                  pl.BlockSpec(memory_space=pl.ANY),
                  pl.BlockSpec((1, d), lambda i, j, st: (0, 0)),
                  pl.BlockSpec((1, 1, d), lambda i, j, st: (i, 0, 0))],
        out_specs=pl.BlockSpec((1, TC, d), lambda i, j, st: (i, j, 0)),
        scratch_shapes=[pltpu.VMEM((N_SLOTS, N_EXPERTS * WIN, d), BF16), pltpu.SemaphoreType.DMA((N_SLOTS,))],
    )
    return pl.pallas_call(
        functools.partial(_combine_kernel, nt=nt, cap=cap),
        out_shape=jax.ShapeDtypeStruct((b, l, d), F32),
        grid_spec=grid_spec,
        compiler_params=pltpu.CompilerParams(dimension_semantics=("arbitrary", "arbitrary"),
                                             vmem_limit_bytes=VMEM_LIMIT),
        name="combine",
    )(starts, x1, pos_st, aff_st, y, nw, g2)


def _route_kernel(aff_ref, pstrict_ref, pos_ref, idx_ref, st_ref, *, cap, nblk):
    rows = N_EXPERTS * nblk
    aff = aff_ref[0]

    def expert_total(x):
        t = jnp.sum(x.reshape(N_EXPERTS, nblk, LANES), axis=(1, 2), keepdims=True)
        return jnp.broadcast_to(t, (N_EXPERTS, nblk, LANES)).reshape(rows, LANES)

    def at_least_cap(cand):
        return expert_total((aff >= cand).astype(F32)) >= cap

    def search_bits(i, thr):
        cand = thr | lax.shift_left(jnp.int32(1), 30 - i)
        return jnp.where(at_least_cap(lax.bitcast_convert_type(cand, F32)), cand, thr)

    thr = lax.fori_loop(0, 31, search_bits, jnp.zeros((rows, LANES), jnp.int32))

    def search_mid(i, lo_hi):
        lo, hi = lo_hi
        mid = 0.5 * (lo + hi)
        ok = at_least_cap(mid)
        return jnp.where(ok, mid, lo), jnp.where(ok, hi, mid)

    lo, hi = lax.fori_loop(0, 24, search_mid,
                           (lax.bitcast_convert_type(thr, F32), lax.bitcast_convert_type(thr + 1, F32)))
    gt = aff >= hi
    eq = jnp.logical_and(aff >= lo, jnp.logical_not(gt))
    need = cap - expert_total(gt.astype(F32))

    r_i = lax.broadcasted_iota(jnp.int32, (LANES, LANES), 0)
    c_i = lax.broadcasted_iota(jnp.int32, (LANES, LANES), 1)
    upper = (r_i <= c_i).astype(BF16)
    ones = jnp.ones((LANES, LANES), BF16)

    def prefix(mask_bf):
        local = _dot(mask_bf, upper)
        tot = _dot(mask_bf, ones)
        return local, tot, _dot(pstrict_ref[...], tot.astype(BF16))

    eq_bf = eq.astype(BF16)
    l_eq, _, o_eq = prefix(eq_bf)
    sel = jnp.logical_or(gt, jnp.logical_and(eq, l_eq + o_eq - eq_bf.astype(F32) < need))
    l_sel, t_sel, o_sel = prefix(sel.astype(BF16))
    pos_ref[0] = jnp.where(sel, (l_sel + o_sel).astype(jnp.int32) - 1, -(1 << 20))
    st_ref[0] = o_sel.astype(jnp.int32)

    p_row = lax.broadcasted_iota(jnp.int32, (1, cap), 1).astype(F32)
    j_col = lax.broadcasted_iota(jnp.int32, (nblk, 1), 0).astype(F32)
    for e in range(N_EXPERTS):
        rs = slice(e * nblk, (e + 1) * nblk)
        off_c = o_sel[rs, 0:1]
        tot_c = t_sel[rs, 0:1]
        inblk = jnp.logical_and(off_c <= p_row, p_row < off_c + tot_c)
        q_row = p_row - jnp.sum(jnp.where(inblk, off_c, 0.0), axis=0, keepdims=True)
        j_row = jnp.sum(jnp.where(inblk, j_col, 0.0), axis=0, keepdims=True)
        l_t = jnp.concatenate([l_sel[rs, :], jnp.zeros((LANES - nblk, LANES), F32)], axis=0).T.astype(BF16)
        in_pad = jnp.concatenate([inblk.astype(BF16), jnp.zeros((LANES - nblk, cap), BF16)], axis=0)
        counts = _dot(l_t, in_pad)
        r_row = jnp.sum((counts <= q_row).astype(F32), axis=0, keepdims=True)
        idx_row = (j_row * LANES + r_row).astype(jnp.int32)
        for a in range(cap // LANES):
            idx_ref[0, e, a:a + 1, :] = idx_row[:, a * LANES:(a + 1) * LANES]


def _route(aff_st, cap):
    b, rows, _ = aff_st.shape
    nblk = rows // N_EXPERTS
    assert nblk <= LANES and cap % LANES == 0
    rho = jnp.arange(rows)
    pstrict = jnp.logical_and(rho[:, None] // nblk == rho[None, :] // nblk, rho[None, :] < rho[:, None]).astype(BF16)
    stacked = pl.BlockSpec((1, rows, LANES), lambda i: (i, 0, 0))
    pos_st, idx, st = pl.pallas_call(
        functools.partial(_route_kernel, cap=cap, nblk=nblk),
        out_shape=(jax.ShapeDtypeStruct((b, rows, LANES), jnp.int32),
                   jax.ShapeDtypeStruct((b, N_EXPERTS, cap // LANES, LANES), jnp.int32),
                   jax.ShapeDtypeStruct((b, rows, LANES), jnp.int32)),
        grid=(b,),
        in_specs=[stacked, pl.BlockSpec((rows, rows), lambda i: (0, 0))],
        out_specs=(stacked, pl.BlockSpec((1, N_EXPERTS, cap // LANES, LANES), lambda i: (i, 0, 0, 0)), stacked),
        compiler_params=pltpu.CompilerParams(dimension_semantics=("arbitrary",), vmem_limit_bytes=VMEM_LIMIT),
        name="route",
    )(aff_st, pstrict)
    starts = jnp.transpose(st[:, :, 0].reshape(b, N_EXPERTS, nblk), (0, 2, 1))
    starts = jnp.concatenate([starts, jnp.full((b, 1, N_EXPERTS), cap, jnp.int32)], axis=1)
    return idx.reshape(b * N_EXPERTS, 1, cap), pos_st, starts.reshape(-1)


def kernel(x, c, ctx, c_ctx, ada_w, ada_b, norm_w, w_in, ssd_conv_w, ssd_conv_b, ssd_dt_bias, ssd_a_log, ssd_d,
           ssd_norm_w, hgrn_lb, hgrn_norm_w, w_out, w_router, w_gate, w_up, w_down):
    b, l, d = x.shape
    lc = ctx.shape[1]
    layer = 0
    assert ada_w.shape[0] == 1, "single-layer stack"
    assert l % TM == 0 and lc % Q == 0

    cvec = jnp.zeros((SUBLANES, d), F32).at[:b].set(c).at[b].set(c_ctx)
    mod = _ada(cvec, ada_w[layer], ada_b[layer][None, :])
    ml = mod[:b].reshape(b, 1, 6, d)
    mc = jnp.broadcast_to(mod[b].reshape(1, 1, 6, d), (b, 1, 6, d))
    nw = norm_w[layer]

    wi = w_in[layer]
    o_xbc, o_dt, o_q = SSD_WIDTH, SSD_WIDTH + XBC, SSD_WIDTH + XBC + 2 * SSD_HEADS
    sec = lambda k: wi[:, o_q + k * 512:o_q + (k + 1) * 512]
    w_r = jnp.concatenate([wi[:, o_xbc:o_dt], sec(0), sec(3), wi[:, :SSD_WIDTH], sec(4), sec(1), sec(2),
                           jnp.pad(wi[:, o_dt:o_q], ((0, 0), (0, LANES - 2 * SSD_HEADS)))], axis=1).astype(BF16)
    pad16 = lambda v: jnp.pad(v.reshape(1, 2 * SSD_HEADS), ((0, 0), (0, LANES - 2 * SSD_HEADS)))
    dtb = pad16(ssd_dt_bias[layer])
    alog = pad16(ssd_a_log[layer])
    lane_id = jnp.arange(LANES)[None, :, None]
    chan_head = (jnp.arange(SSD_WIDTH) // SSD_HEADDIM)[None, None, :]
    expm = (lane_id == chan_head + SSD_HEADS * jnp.arange(2)[:, None, None]).astype(BF16)
    lbs = jnp.cumsum(jax.nn.softmax(hgrn_lb.astype(F32), axis=0), axis=0)[layer]
    cw, cb = ssd_conv_w[layer], ssd_conv_b[layer][None, :]

    ca, cbb = _inproj(ctx, mc[:, :, 0], mc[:, :, 1], nw[0:1], w_r, cw, cb, dtb, lbs[0:1], lbs[1:2], lc)
    ua, ub = _inproj(x, ml[:, :, 0], ml[:, :, 1], nw[0:1], w_r, cw, cb, dtb, lbs[0:1], lbs[1:2], GRID_W)

    s0s = jnp.zeros((b, 2, SSD_GROUPS, GW, SSD_STATE), F32)
    s0h = jnp.zeros((b, 2, HGRN_HEADS, HGRN_HEADDIM, HGRN_HEADDIM), F32)
    scs, sch = _mix(ca, cbb, alog, expm, s0s, s0h, emit_y=False)
    yf, yb, _, _ = _mix(ua, ub, alog, expm, scs, sch, emit_y=True)

    dexp = jnp.repeat(ssd_d[layer].astype(F32), SSD_HEADDIM)[None, :]
    x1, h2r, aff = _outproj(x, yf, yb, ua, dexp, ssd_norm_w[layer][None, :], hgrn_norm_w[layer][None, :],
                            w_out[layer].astype(BF16), nw[1:2], nw[2:3], ml[:, :, 2], ml[:, :, 3], ml[:, :, 4],
                            w_router[layer].T)

    cap = CAPACITY_FACTOR * l // N_EXPERTS
    assert cap % RC == 0 and cap >= WIN
    idx, pos_st, starts = _route(aff, cap)
    y = _expert(idx, h2r.reshape(b * l, SUBLANES, LANES), w_gate[layer], w_up[layer], w_down[layer], b, l)
    return _combine(starts, x1, pos_st, aff, y, nw[3:4], ml[:, :, 5])
```
